```python
import math
import jax, jax.numpy as jnp
from jax import lax
import numpy as np

D_MODEL = 1024
BATCH = 4
SEQ = 8192
DEPTH = 2
DEC_BATCH = 32
DEC_SEQ = 32
PAST_LEN = 1024

CHUNK = 64
N_A = DEPTH // 2
N_B = DEPTH - N_A
D_MIX = D_MODEL
CONV_W = 3
N_HEADS = 16
HEAD_DIM = D_MODEL // N_HEADS
D_ATT = N_HEADS * HEAD_DIM
D_FF = 2816
Q_BLOCK = 128
EPS = 1e-6

kernel_name = "yoco_shortconv_fox_convffn_step"


def _rmsnorm(x, g):
    xf = x.astype(jnp.float32)
    y = xf * lax.rsqrt(jnp.mean(xf * xf, axis=-1, keepdims=True) + EPS)
    return (y * g.astype(jnp.float32)).astype(x.dtype)


def _causal_dwconv(x, ctx, w):
    T = x.shape[1]
    xp = jnp.concatenate([ctx.astype(x.dtype), x], axis=1)
    y = xp[:, 0:T] * w[0]
    for i in range(1, CONV_W):
        y = y + xp[:, i:i + T] * w[i]
    return y, xp[:, T:]


def _mixer_a(h, ctx, w_in, conv_w, w_out):
    gb, gc, u = jnp.split(h @ w_in, 3, axis=-1)
    z, new_ctx = _causal_dwconv(gc * u, ctx, conv_w)
    return (gb * z) @ w_out, new_ctx


def _conv_ffn(h, ctx, w_up, conv_w, w_down):
    up = h @ w_up
    upc, new_ctx = _causal_dwconv(up, ctx, conv_w)
    g, v = jnp.split(upc, 2, axis=-1)
    return (jax.nn.silu(g) * v) @ w_down, new_ctx


def _fox_attend(q, k, v, cq, ck, qpos, kpos):
    s = jnp.einsum('bqhd,bkhd->bhqk', q, k).astype(jnp.float32) * (HEAD_DIM ** -0.5)
    s = s + jnp.swapaxes(cq, 1, 2)[..., :, None] - jnp.swapaxes(ck, 1, 2)[..., None, :]
    mask = kpos[None, :] <= qpos[:, None]
    s = jnp.where(mask, s, -jnp.inf)
    p = jax.nn.softmax(s, axis=-1)
    return jnp.einsum('bhqk,bkhd->bqhd', p.astype(v.dtype), v)


def _fox_mix(q, k_all, v_all, logf_all):
    B, T = q.shape[0], q.shape[1]
    Tk = k_all.shape[1]
    c = jnp.cumsum(logf_all.astype(jnp.float32), axis=1)
    cq = c[:, Tk - T:]
    kpos = jnp.arange(Tk)
    qpos = jnp.arange(Tk - T, Tk)
    if T % Q_BLOCK == 0:
        nb = T // Q_BLOCK
        qb = jnp.moveaxis(q.reshape(B, nb, Q_BLOCK, N_HEADS, HEAD_DIM), 1, 0)
        cqb = jnp.moveaxis(cq.reshape(B, nb, Q_BLOCK, N_HEADS), 1, 0)
        pb = qpos.reshape(nb, Q_BLOCK)
        ob = lax.map(lambda a: _fox_attend(a[0], k_all, v_all, a[1], c, a[2], kpos), (qb, cqb, pb))
        return jnp.moveaxis(ob, 0, 1).reshape(q.shape)
    return _fox_attend(q, k_all, v_all, cq, c, qpos, kpos)


def _trunk(x, sa, sf, past_k, past_v, past_logf,
           a_norm, w_a_in, a_conv_w, w_a_out, kv_norm, w_kv, b_f, b_norm, w_q, w_o,
           ffn_norm, w_ffn_up, ffn_conv_w, w_ffn_down, final_norm):
    B, T, _ = x.shape
    new_sa, new_sf = [], []
    k_new = v_new = logf_new = None
    k_all = v_all = logf_all = None
    for l in range(DEPTH):
        if l < N_A:
            y, st = _mixer_a(_rmsnorm(x, a_norm[l]), sa[l], w_a_in[l], a_conv_w[l], w_a_out[l])
            new_sa.append(st)
        else:
            j = l - N_A
            if j == 0:
                kvf = _rmsnorm(x, kv_norm) @ w_kv
                k_new = kvf[..., :D_ATT].reshape(B, T, N_HEADS, HEAD_DIM)
                v_new = kvf[..., D_ATT:2 * D_ATT].reshape(B, T, N_HEADS, HEAD_DIM)
                logf_new = jax.nn.log_sigmoid(kvf[..., 2 * D_ATT:].astype(jnp.float32)
                                              + b_f.astype(jnp.float32))
                if past_k is None:
                    k_all, v_all, logf_all = k_new, v_new, logf_new
                else:
                    k_all = jnp.concatenate([past_k.astype(k_new.dtype), k_new], axis=1)
                    v_all = jnp.concatenate([past_v.astype(v_new.dtype), v_new], axis=1)
                    logf_all = jnp.concatenate([past_logf.astype(jnp.float32), logf_new], axis=1)
            q = (_rmsnorm(x, b_norm[j]) @ w_q[j]).reshape(B, T, N_HEADS, HEAD_DIM)
            o = _fox_mix(q, k_all, v_all, logf_all)
            y = o.reshape(B, T, D_ATT) @ w_o[j]
        x = x + y
        f, st = _conv_ffn(_rmsnorm(x, ffn_norm[l]), sf[l], w_ffn_up[l], ffn_conv_w[l], w_ffn_down[l])
        new_sf.append(st)
        x = x + f
    return (_rmsnorm(x, final_norm), jnp.stack(new_sa), jnp.stack(new_sf),
            k_new, v_new, logf_new.astype(x.dtype))


def setup_inputs(seed: int = 0) -> dict:
    key = jax.random.key(seed)
    ks = jax.random.split(key, 24)
    f32 = jnp.float32
    nrm = lambda k, shape, s: jax.random.normal(k, shape, f32) * s
    gain = lambda k, shape: 1.0 + 0.01 * jax.random.normal(k, shape, f32)
    w_kv = nrm(ks[9], (D_MODEL, 2 * D_ATT + N_HEADS), D_MODEL ** -0.5)
    w_kv = w_kv.at[:, 2 * D_ATT:].multiply(0.1)
    return {
        "x_prompt": nrm(ks[0], (BATCH, SEQ, D_MODEL), 1.0),
        "x_sample": nrm(ks[1], (DEC_BATCH, DEC_SEQ, D_MODEL), 1.0),
        "state_conv_a": nrm(ks[2], (N_A, DEC_BATCH, CONV_W - 1, D_MIX), 1.0),
        "state_ffn_conv": nrm(ks[3], (DEPTH, DEC_BATCH, CONV_W - 1, 2 * D_FF), 1.0),
        "cache_k": nrm(ks[4], (DEC_BATCH, PAST_LEN, N_HEADS, HEAD_DIM), 1.0),
        "cache_v": nrm(ks[5], (DEC_BATCH, PAST_LEN, N_HEADS, HEAD_DIM), 1.0),
        "cache_logf": jax.nn.log_sigmoid(4.0 + 0.1 * jax.random.normal(ks[6], (DEC_BATCH, PAST_LEN, N_HEADS), f32)),
        "a_norm": gain(ks[7], (N_A, D_MODEL)),
        "w_a_in": nrm(ks[8], (N_A, D_MODEL, 3 * D_MIX), D_MODEL ** -0.5),
        "a_conv_w": nrm(ks[10], (N_A, CONV_W, D_MIX), CONV_W ** -0.5),
        "w_a_out": nrm(ks[11], (N_A, D_MIX, D_MODEL), D_MIX ** -0.5),
        "kv_norm": gain(ks[12], (D_MODEL,)),
        "w_kv": w_kv,
        "b_f": 4.0 + 0.1 * jax.random.normal(ks[13], (N_HEADS,), f32),
        "b_norm": gain(ks[14], (N_B, D_MODEL)),
        "w_q": nrm(ks[15], (N_B, D_MODEL, D_ATT), D_MODEL ** -0.5),
        "w_o": nrm(ks[16], (N_B, D_ATT, D_MODEL), D_ATT ** -0.5),
        "ffn_norm": gain(ks[17], (DEPTH, D_MODEL)),
        "w_ffn_up": nrm(ks[18], (DEPTH, D_MODEL, 2 * D_FF), D_MODEL ** -0.5),
        "ffn_conv_w": nrm(ks[19], (DEPTH, CONV_W, 2 * D_FF), CONV_W ** -0.5),
        "w_ffn_down": nrm(ks[20], (DEPTH, D_FF, D_MODEL), D_FF ** -0.5),
        "final_norm": gain(ks[21], (D_MODEL,)),
    }


def reference(x_prompt, x_sample, state_conv_a, state_ffn_conv, cache_k, cache_v, cache_logf,
              a_norm, w_a_in, a_conv_w, w_a_out, kv_norm, w_kv, b_f, b_norm, w_q, w_o,
              ffn_norm, w_ffn_up, ffn_conv_w, w_ffn_down, final_norm):
    weights = (a_norm, w_a_in, a_conv_w, w_a_out, kv_norm, w_kv, b_f, b_norm, w_q, w_o,
               ffn_norm, w_ffn_up, ffn_conv_w, w_ffn_down, final_norm)
    Bp = x_prompt.shape[0]
    sa0 = jnp.zeros((N_A, Bp, CONV_W - 1, D_MIX), x_prompt.dtype)
    sf0 = jnp.zeros((DEPTH, Bp, CONV_W - 1, 2 * D_FF), x_prompt.dtype)
    y_prompt, p_conv_a, p_ffn_conv, p_k, p_v, p_logf = _trunk(
        x_prompt, sa0, sf0, None, None, None, *weights)
    y_sample, s_conv_a, s_ffn_conv, s_k, s_v, s_logf = _trunk(
        x_sample, state_conv_a, state_ffn_conv, cache_k, cache_v, cache_logf, *weights)
    return (y_prompt, y_sample, p_conv_a, p_ffn_conv, p_k, p_v, p_logf,
            s_conv_a, s_ffn_conv, s_k, s_v, s_logf)
```

```python
import functools

import numpy as np
import jax
import jax.numpy as jnp
from jax import lax
from jax.experimental import pallas as pl
from jax.experimental.pallas import tpu as pltpu

D = 1024
F = 2816
H = 16
DH = 64
NPAIR = H // 2
LANES = 128
CH = 256
NCH = F // CH
EPS = 1e-6
NEG = -1e30
PIECES = 3
VMEM_LIMIT = 56 * 1024 * 1024

f32 = jnp.float32
bf16 = jnp.bfloat16


def _const_spec(shape):
    zeros = (0,) * len(shape)
    return pl.BlockSpec(shape, lambda *_: zeros, pipeline_mode=pl.Buffered(1))


def _params(sem):
    return pltpu.CompilerParams(dimension_semantics=sem, vmem_limit_bytes=VMEM_LIMIT)


def _rms(x, g):
    ms = jnp.mean(x * x, axis=-1, keepdims=True)
    return (x * lax.rsqrt(ms + EPS)) * g


def _dot(a, b):
    return jnp.dot(a, b, preferred_element_type=f32)


def _dot_nt(a, b):
    return lax.dot_general(a, b, (((1,), (1,)), ((), ())), preferred_element_type=f32)


def _mix_kernel(x_ref, ctx_ref, g_ref, win_ref, cw_ref, wout_ref, y_ref, nctx_ref, cb_ref, *, nb, tt):
    j = pl.program_id(1)
    m = nb * tt
    x = x_ref[...].reshape(m, D)
    h = _rms(x, g_ref[...]).astype(bf16)
    gb = _dot(h, win_ref[:, 0:D])
    cu = _dot(h, win_ref[:, D:2 * D]) * _dot(h, win_ref[:, 2 * D:3 * D])

    @pl.when(j == 0)
    def _():
        cb_ref[:, 6:8, :] = ctx_ref[...]

    @pl.when(j > 0)
    def _():
        cb_ref[:, 6:8, :] = cb_ref[:, 6 + tt:8 + tt, :]

    cb_ref[:, 8:8 + tt, :] = cu.reshape(nb, tt, D)
    cw = cw_ref[...]
    z = (cb_ref[:, 6:6 + tt, :] * cw[0:1] + cb_ref[:, 7:7 + tt, :] * cw[1:2]).reshape(m, D) + cu * cw[2:3]
    y = _dot((gb * z).astype(bf16), wout_ref[...])
    y_ref[...] = (x + y).reshape(nb, tt, D)
    nctx_ref[...] = cb_ref[:, 6 + tt:8 + tt, :]


def _mixer(x, ctx, g, w_in, cw, w_out, *, nb, tt):
    b, t, _ = x.shape
    grid = (b // nb, t // tt)
    return pl.pallas_call(
        functools.partial(_mix_kernel, nb=nb, tt=tt),
        grid=grid,
        in_specs=[
            pl.BlockSpec((nb, tt, D), lambda i, j: (i, j, 0)),
            pl.BlockSpec((nb, 2, D), lambda i, j: (i, 0, 0)),
            _const_spec((1, D)),
            _const_spec((D, 3 * D)),
            _const_spec((3, D)),
            _const_spec((D, D)),
        ],
        out_specs=[
            pl.BlockSpec((nb, tt, D), lambda i, j: (i, j, 0)),
            pl.BlockSpec((nb, 2, D), lambda i, j: (i, 0, 0)),
        ],
        out_shape=[jax.ShapeDtypeStruct((b, t, D), f32), jax.ShapeDtypeStruct((b, 2, D), f32)],
        scratch_shapes=[pltpu.VMEM((nb, 8 + tt, D), f32)],
        compiler_params=_params(("parallel", "arbitrary")),
        name="mixer_a",
    )(x, ctx, g, w_in, cw, w_out)


def _ffn_kernel(*refs, nb, tt, with_attn, with_final):
    refs = list(refs)
    x_ref = refs.pop(0)
    if with_attn:
        o_ref, wo_ref = refs.pop(0), refs.pop(0)
    ctx_ref, g_ref, wup_ref, cw_ref, wdown_ref = (refs.pop(0) for _ in range(5))
    if with_final:
        fg_ref = refs.pop(0)
    y_ref, nctx_ref, acc_ref, st_ref, cb_ref = refs

    j = pl.program_id(1)
    m = nb * tt
    x = x_ref[...].reshape(m, D)
    if with_attn:
        x = x + _dot(o_ref[...].reshape(m, D), wo_ref[...])
    h = _rms(x, g_ref[...]).astype(bf16)

    @pl.when(j == 0)
    def _():
        st_ref[:, 0:2, :] = ctx_ref[...]

    acc_ref[...] = x

    def conv(u, lo, slot):
        hi = lo + CH
        cb_ref[slot, :, 6:8, :] = st_ref[:, 0:2, lo:hi]
        cb_ref[slot, :, 8:8 + tt, :] = u.reshape(nb, tt, CH)
        cw = cw_ref[:, lo:hi]
        y = (cb_ref[slot, :, 6:6 + tt, :] * cw[0:1] + cb_ref[slot, :, 7:7 + tt, :] * cw[1:2]).reshape(m, CH)
        st_ref[:, 0:2, lo:hi] = cb_ref[slot, :, 6 + tt:8 + tt, :]
        return y + u * cw[2:3]

    for c in range(NCH):
        glo, vlo = c * CH, F + c * CH
        gate = conv(_dot(h, wup_ref[:, glo:glo + CH]), glo, 0)
        val = conv(_dot(h, wup_ref[:, vlo:vlo + CH]), vlo, 1)
        act = (gate * (1.0 / (1.0 + jnp.exp(-gate))) * val).astype(bf16)
        acc_ref[...] += _dot(act, wdown_ref[glo:glo + CH, :])

    y = acc_ref[...]
    if with_final:
        y = _rms(y, fg_ref[...])
    y_ref[...] = y.reshape(nb, tt, D)
    nctx_ref[...] = st_ref[:, 0:2, :]


def _ffn(x, ctx, g, w_up, cw, w_down, *, nb, tt, attn=None, final_g=None):
    b, t, _ = x.shape
    grid = (b // nb, t // tt)
    tile = pl.BlockSpec((nb, tt, D), lambda i, j: (i, j, 0))
    args, specs = [x], [tile]
    if attn is not None:
        o, w_o = attn
        args += [o, w_o]
        specs += [tile, _const_spec((D, D))]
    args += [ctx, g, w_up, cw, w_down]
    specs += [
        pl.BlockSpec((nb, 2, 2 * F), lambda i, j: (i, 0, 0)),
        _const_spec((1, D)),
        _const_spec((D, 2 * F)),
        _const_spec((3, 2 * F)),
        _const_spec((F, D)),
    ]
    if final_g is not None:
        args.append(final_g)
        specs.append(_const_spec((1, D)))
    return pl.pallas_call(
        functools.partial(_ffn_kernel, nb=nb, tt=tt, with_attn=attn is not None, with_final=final_g is not None),
        grid=grid,
        in_specs=specs,
        out_specs=[tile, pl.BlockSpec((nb, 2, 2 * F), lambda i, j: (i, 0, 0))],
        out_shape=[jax.ShapeDtypeStruct((b, t, D), f32), jax.ShapeDtypeStruct((b, 2, 2 * F), f32)],
        scratch_shapes=[
            pltpu.VMEM((nb * tt, D), f32),
            pltpu.VMEM((nb, 8, 2 * F), f32),
            pltpu.VMEM((2, nb, 8 + tt, CH), f32),
        ],
        compiler_params=_params(("parallel", "arbitrary")),
        name="conv_ffn_attn" if attn is not None else "conv_ffn",
    )(*args)


def _split3(c):
    hi = c.astype(bf16).astype(f32)
    r = c - hi
    mid = r.astype(bf16).astype(f32)
    lo = (r - mid).astype(bf16).astype(f32)
    return hi, mid, lo


def _pack3(c):
    hi, mid, lo = _split3(c)
    return hi + pltpu.roll(mid, H, 1) + pltpu.roll(lo, 2 * H, 1)


def _cumsum_cols(logf, carry, tri, eq_ref, ek_ref, oq_ref, ok_ref):
    lane = lax.broadcasted_iota(jnp.int32, logf.shape, 1)
    head = lane < H
    lf = jnp.where(head, logf, 0.0)
    cs = _dot(tri, _pack3(lf).astype(bf16))
    c = cs + pltpu.roll(cs, LANES - H, 1) + pltpu.roll(cs, LANES - 2 * H, 1)
    c = jnp.where(head, c, 0.0) + carry
    cp = _pack3(c).astype(bf16)
    qa = _dot(cp, eq_ref[...]) + oq_ref[...]
    ka = _dot(cp, ek_ref[...]) + ok_ref[...]
    return c, qa.astype(bf16), ka.astype(bf16)


def _tri(nb, tt):
    m = nb * tt
    r = lax.broadcasted_iota(jnp.int32, (m, m), 0)
    c = lax.broadcasted_iota(jnp.int32, (m, m), 1)
    if nb == 1:
        keep = c <= r
    else:
        assert tt & (tt - 1) == 0, "several streams per tile need a power-of-two tile length"
        sh = tt.bit_length() - 1
        keep = (c <= r) & ((r >> sh) == (c >> sh))
    return jnp.where(keep, 1.0, 0.0).astype(bf16)


def _aug_constants():
    eq = np.zeros((LANES, D), np.float32)
    ek = np.zeros((LANES, D), np.float32)
    oq = np.zeros((1, D), np.float32)
    ok = np.zeros((1, D), np.float32)
    for h in range(H):
        base = (h // 2) * LANES + (h % 2) * 2 * PIECES
        for x in range(PIECES):
            eq[x * H + h, base + x] = 1.0
            ok[0, base + x] = 1.0
            oq[0, base + PIECES + x] = 1.0
            ek[x * H + h, base + PIECES + x] = -1.0
    return jnp.asarray(eq, bf16), jnp.asarray(ek, bf16), jnp.asarray(oq), jnp.asarray(ok)


def _proj_kernel(x_ref, c0_ref, kvg_ref, qg_ref, wk_ref, wv_ref, wf_ref, bf_ref, wq_ref,
                 eq_ref, ek_ref, oq_ref, ok_ref,
                 k_ref, v_ref, lf_ref, qm_ref, qa_ref, km_ref, ka_ref, vm_ref, carry_ref, *, nb, tt):
    j = pl.program_id(1)
    m = nb * tt
    x = x_ref[...].reshape(m, D)
    xn = x * lax.rsqrt(jnp.mean(x * x, axis=-1, keepdims=True) + EPS)
    hk = (xn * kvg_ref[...]).astype(bf16)
    hq = (xn * qg_ref[...]).astype(bf16)

    k = _dot(hk, wk_ref[...])
    v = _dot(hk, wv_ref[...])
    k_ref[...] = k.reshape(nb, tt, D)
    v_ref[...] = v.reshape(nb, tt, D)
    km_ref[...] = k.astype(bf16).reshape(nb, tt, D)
    vm_ref[...] = v.astype(bf16).reshape(nb, tt, D)
    qm_ref[...] = _dot(hq, wq_ref[...]).astype(bf16).reshape(nb, tt, D)

    logf = jax.nn.log_sigmoid(_dot(hk, wf_ref[...]) + bf_ref[...])
    lf_ref[...] = logf[:, 0:H].reshape(nb, tt, H)

    @pl.when(j == 0)
    def _():
        carry_ref[...] = c0_ref[...]

    carry = jnp.broadcast_to(carry_ref[...], (nb, tt, LANES)).reshape(m, LANES)
    c, qa, ka = _cumsum_cols(logf, carry, _tri(nb, tt), eq_ref, ek_ref, oq_ref, ok_ref)
    qa_ref[...] = qa.reshape(nb, tt, D)
    ka_ref[...] = ka.reshape(nb, tt, D)
    carry_ref[...] = c.reshape(nb, tt, LANES)[:, tt - 1:tt, :]


def _proj(x, c0, kvg, qg, wk, wv, wf, bfp, wq, consts, *, nb, tt):
    b, t, _ = x.shape
    grid = (b // nb, t // tt)
    tile = pl.BlockSpec((nb, tt, D), lambda i, j: (i, j, 0))
    wide = jax.ShapeDtypeStruct((b, t, D), f32)
    half = jax.ShapeDtypeStruct((b, t, D), bf16)
    return pl.pallas_call(
        functools.partial(_proj_kernel, nb=nb, tt=tt),
        grid=grid,
        in_specs=[
            tile,
            pl.BlockSpec((nb, 1, LANES), lambda i, j: (i, 0, 0)),
            _const_spec((1, D)), _const_spec((1, D)),
            _const_spec((D, D)), _const_spec((D, D)), _const_spec((D, LANES)), _const_spec((1, LANES)),
            _const_spec((D, D)),
            _const_spec((LANES, D)), _const_spec((LANES, D)), _const_spec((1, D)), _const_spec((1, D)),
        ],
        out_specs=[tile, tile, pl.BlockSpec((nb, tt, H), lambda i, j: (i, j, 0)), tile, tile, tile, tile, tile],
        out_shape=[wide, wide, jax.ShapeDtypeStruct((b, t, H), f32), half, half, half, half, half],
        scratch_shapes=[pltpu.VMEM((nb, 1, LANES), f32)],
        compiler_params=_params(("parallel", "arbitrary")),
        name="kvq_proj",
    )(x, c0, kvg, qg, wk, wv, wf, bfp, wq, *consts)


def _cache_aug_kernel(lf_ref, place_ref, eq_ref, ek_ref, oq_ref, ok_ref, ka_ref, ctot_ref, *, p):
    lf16 = lf_ref[0]
    hi, mid, lo = _split3(lf16)
    place = place_ref[...]
    logf = _dot(hi.astype(bf16), place) + _dot(mid.astype(bf16), place) + _dot(lo.astype(bf16), place)
    c, _, ka = _cumsum_cols(logf, jnp.zeros((1, LANES), f32), _tri(1, p), eq_ref, ek_ref, oq_ref, ok_ref)
    ka_ref[0] = ka
    ctot_ref[0] = c[p - 1:p, :]


def _cache_aug(cache_logf, consts):
    b, p, _ = cache_logf.shape
    place = jnp.asarray(np.eye(H, LANES, dtype=np.float32), bf16)
    return pl.pallas_call(
        functools.partial(_cache_aug_kernel, p=p),
        grid=(b,),
        in_specs=[
            pl.BlockSpec((1, p, H), lambda i: (i, 0, 0)),
            _const_spec((H, LANES)),
            _const_spec((LANES, D)), _const_spec((LANES, D)), _const_spec((1, D)), _const_spec((1, D)),
        ],
        out_specs=[pl.BlockSpec((1, p, D), lambda i: (i, 0, 0)), pl.BlockSpec((1, 1, LANES), lambda i: (i, 0, 0))],
        out_shape=[jax.ShapeDtypeStruct((b, p, D), bf16), jax.ShapeDtypeStruct((b, 1, LANES), f32)],
        compiler_params=_params(("parallel",)),
        name="cache_aug",
    )(cache_logf, place, *consts)


def _stack_heads(qm, qa):
    t = qm.shape[0]
    q2 = jnp.concatenate([qm, qa], axis=1).astype(f32)
    lane = lax.broadcasted_iota(jnp.int32, (t, 2 * LANES), 1)
    grp = 2 * PIECES
    keep_a = (lane < DH) | ((lane >= LANES) & (lane < LANES + grp))
    keep_b = ((lane >= DH) & (lane < LANES)) | ((lane >= LANES + grp) & (lane < LANES + 2 * grp))
    return jnp.concatenate([jnp.where(keep_a, q2, 0.0), jnp.where(keep_b, q2, 0.0)], axis=0).astype(bf16)


def _unstack_heads(o, t):
    lane = lax.broadcasted_iota(jnp.int32, (t, LANES), 1)
    return jnp.where(lane < DH, o[0:t], o[t:2 * t])


def _causal(s, t):
    row = lax.broadcasted_iota(jnp.int32, s.shape, 0)
    col = lax.broadcasted_iota(jnp.int32, s.shape, 1)
    row = jnp.where(row >= t, row - t, row)
    return jnp.where(col <= row, s, NEG)


def _attn_prompt_kernel(qm_ref, qa_ref, km_ref, ka_ref, vm_ref, o_ref, m_ref, l_ref, acc_ref, *, tq):
    i = pl.program_id(2)
    qs = _stack_heads(qm_ref[0], qa_ref[0])
    m_ref[...] = jnp.full(m_ref.shape, NEG, f32)
    l_ref[...] = jnp.zeros(l_ref.shape, f32)
    acc_ref[...] = jnp.zeros(acc_ref.shape, f32)

    def step(jb, masked):
        off = pl.multiple_of(jb * tq, tq)
        kblk = jnp.concatenate([km_ref[0, pl.ds(off, tq), :], ka_ref[0, pl.ds(off, tq), :]], axis=1)
        s = _dot_nt(qs, kblk)
        if masked:
            s = _causal(s, tq)
        m_prev = m_ref[...]
        m_new = jnp.maximum(m_prev, jnp.max(s, axis=1, keepdims=True))
        alpha = jnp.exp(m_prev - m_new)
        p = jnp.exp(s - m_new)
        l_ref[...] = alpha * l_ref[...] + jnp.sum(p, axis=1, keepdims=True)
        acc_ref[...] = alpha * acc_ref[...] + _dot(p.astype(bf16), vm_ref[0, pl.ds(off, tq), :])
        m_ref[...] = m_new

    def body(jb, carry):
        step(jb, False)
        return carry

    lax.fori_loop(0, i, body, 0)
    step(i, True)
    o = acc_ref[...] * (1.0 / l_ref[...])
    o_ref[0] = _unstack_heads(o, tq).astype(bf16)


def _attn_prompt(qm, qa, km, ka, vm, *, tq):
    b, t, _ = qm.shape
    qspec = pl.BlockSpec((1, tq, LANES), lambda bi, p, i: (bi, i, p))
    kspec = pl.BlockSpec((1, t, LANES), lambda bi, p, i: (bi, 0, p))
    return pl.pallas_call(
        functools.partial(_attn_prompt_kernel, tq=tq),
        grid=(b, NPAIR, t // tq),
        in_specs=[qspec, qspec, kspec, kspec, kspec],
        out_specs=qspec,
        out_shape=jax.ShapeDtypeStruct((b, t, D), bf16),
        scratch_shapes=[
            pltpu.VMEM((2 * tq, 1), f32),
            pltpu.VMEM((2 * tq, 1), f32),
            pltpu.VMEM((2 * tq, LANES), f32),
        ],
        compiler_params=_params(("parallel", "parallel", "arbitrary")),
        name="fox_attn_prompt",
    )(qm, qa, km, ka, vm)


def _attn_sample_kernel(qm_ref, qa_ref, kc_ref, kca_ref, vc_ref, kn_ref, kna_ref, vn_ref, o_ref, *, t):
    qs = _stack_heads(qm_ref[0], qa_ref[0])
    kc = jnp.concatenate([kc_ref[0].astype(bf16), kca_ref[0]], axis=1)
    kn = jnp.concatenate([kn_ref[0], kna_ref[0]], axis=1)
    s_c = _dot_nt(qs, kc)
    s_n = _causal(_dot_nt(qs, kn), t)
    m = jnp.maximum(jnp.max(s_c, axis=1, keepdims=True), jnp.max(s_n, axis=1, keepdims=True))
    p_c = jnp.exp(s_c - m)
    p_n = jnp.exp(s_n - m)
    l = jnp.sum(p_c, axis=1, keepdims=True) + jnp.sum(p_n, axis=1, keepdims=True)
    o = _dot(p_c.astype(bf16), vc_ref[0].astype(bf16)) + _dot(p_n.astype(bf16), vn_ref[0])
    o_ref[0] = _unstack_heads(o * (1.0 / l), t).astype(bf16)


def _attn_sample(qm, qa, kc, kca, vc, kn, kna, vn):
    b, t, _ = qm.shape
    p = kc.shape[1]
    new = pl.BlockSpec((1, t, LANES), lambda bi, pr: (bi, 0, pr))
    old = pl.BlockSpec((1, p, LANES), lambda bi, pr: (bi, 0, pr))
    return pl.pallas_call(
        functools.partial(_attn_sample_kernel, t=t),
        grid=(b, NPAIR),
        in_specs=[new, new, old, old, old, new, new, new],
        out_specs=new,
        out_shape=jax.ShapeDtypeStruct((b, t, D), bf16),
        compiler_params=_params(("parallel", "parallel")),
        name="fox_attn_sample",
    )(qm, qa, kc, kca, vc, kn, kna, vn)


def _trunk(x, sa, sf, past, w, consts, *, nb, tt, tq):
    b, t, _ = x.shape
    x1, nsa = _mixer(x, sa[0], w["a_g"], w["w_in"], w["a_cw"], w["w_out"], nb=nb, tt=tt)
    x2, nsf0 = _ffn(x1, sf[0], w["f_g"][0], w["w_up"][0], w["f_cw"][0], w["w_down"][0], nb=nb, tt=tt)
    if past is None:
        c0 = jnp.zeros((b, 1, LANES), f32)
    else:
        cache_k, cache_v, cache_logf = past
        kca, c0 = _cache_aug(cache_logf, consts)
    k, v, logf, qm, qa, km, ka, vm = _proj(x2, c0, w["kv_g"], w["q_g"], w["w_k"], w["w_v"], w["w_f"], w["b_f"],
                                           w["w_q"], consts, nb=nb, tt=tt)
    if past is None:
        o = _attn_prompt(qm, qa, km, ka, vm, tq=tq)
    else:
        p = cache_k.shape[1]
        o = _attn_sample(qm, qa, cache_k.reshape(b, p, D), kca, cache_v.reshape(b, p, D), km, ka, vm)
    y, nsf1 = _ffn(x2, sf[1], w["f_g"][1], w["w_up"][1], w["f_cw"][1], w["w_down"][1], nb=nb, tt=tt,
                   attn=(o, w["w_o"]), final_g=w["final_g"])
    return (y, nsa[None], jnp.stack([nsf0, nsf1]), k.reshape(b, t, H, DH), v.reshape(b, t, H, DH), logf)


def kernel(x_prompt, x_sample, state_conv_a, state_ffn_conv, cache_k, cache_v, cache_logf, a_norm, w_a_in,
           a_conv_w, w_a_out, kv_norm, w_kv, b_f, b_norm, w_q, w_o, ffn_norm, w_ffn_up, ffn_conv_w,
           w_ffn_down, final_norm):
    assert a_norm.shape[0] == 1 and b_norm.shape[0] == 1 and ffn_norm.shape[0] == 2
    w = {
        "a_g": a_norm[0][None], "w_in": w_a_in[0].astype(bf16), "a_cw": a_conv_w[0],
        "w_out": w_a_out[0].astype(bf16),
        "f_g": [ffn_norm[l][None] for l in range(2)],
        "w_up": [w_ffn_up[l].astype(bf16) for l in range(2)],
        "f_cw": [ffn_conv_w[l] for l in range(2)],
        "w_down": [w_ffn_down[l].astype(bf16) for l in range(2)],
        "kv_g": kv_norm[None], "q_g": b_norm[0][None],
        "w_k": w_kv[:, 0:D].astype(bf16), "w_v": w_kv[:, D:2 * D].astype(bf16),
        "w_f": jnp.pad(w_kv[:, 2 * D:], ((0, 0), (0, LANES - H))).astype(bf16),
        "b_f": jnp.pad(b_f, (0, LANES - H))[None],
        "w_q": (w_q[0] * (DH ** -0.5)).astype(bf16),
        "w_o": w_o[0].astype(bf16), "final_g": final_norm[None],
    }
    consts = _aug_constants()
    bp = x_prompt.shape[0]
    sa0 = jnp.zeros((1, bp, 2, D), f32)
    sf0 = jnp.zeros((2, bp, 2, 2 * F), f32)
    outs_p = _trunk(x_prompt, sa0, sf0, None, w, consts, nb=1, tt=512, tq=512)
    outs_s = _trunk(x_sample, state_conv_a, state_ffn_conv, (cache_k, cache_v, cache_logf), w, consts,
                    nb=8, tt=x_sample.shape[1], tq=None)
    y_p, p_a, p_f, p_k, p_v, p_lf = outs_p
    y_s, s_a, s_f, s_k, s_v, s_lf = outs_s
    return (y_p, y_s, p_a, p_f, p_k, p_v, p_lf, s_a, s_f, s_k, s_v, s_lf)
```

```python
import functools

import numpy as np
import jax
import jax.numpy as jnp
from jax import lax
from jax.experimental import pallas as pl
from jax.experimental.pallas import tpu as pltpu

D = 1024
F = 2816
H = 16
DH = 64
NPAIR = H // 2
LANES = 128
CH = 256
NCH = F // CH
EPS = 1e-6
NEG = -1e30
PIECES = 3
LOG2E = 1.4426950408889634
ONES_ROWS = 16
VMEM_LIMIT = 56 * 1024 * 1024

f32 = jnp.float32
bf16 = jnp.bfloat16


def _const_spec(shape):
    zeros = (0,) * len(shape)
    return pl.BlockSpec(shape, lambda *_: zeros, pipeline_mode=pl.Buffered(1))


def _params(sem):
    return pltpu.CompilerParams(dimension_semantics=sem, vmem_limit_bytes=VMEM_LIMIT)


def _rms(x, g):
    ms = jnp.mean(x * x, axis=-1, keepdims=True)
    return (x * lax.rsqrt(ms + EPS)) * g


def _dot(a, b):
    return jnp.dot(a, b, preferred_element_type=f32)


def _dot_nt(a, b):
    return lax.dot_general(a, b, (((1,), (1,)), ((), ())), preferred_element_type=f32)


def _mix_kernel(x_ref, ctx_ref, g_ref, win_ref, cw_ref, wout_ref, y_ref, nctx_ref, cb_ref, *, nb, tt):
    j = pl.program_id(1)
    m = nb * tt
    x = x_ref[...].reshape(m, D)
    h = _rms(x, g_ref[...]).astype(bf16)
    gb = _dot(h, win_ref[:, 0:D])
    cu = _dot(h, win_ref[:, D:2 * D]) * _dot(h, win_ref[:, 2 * D:3 * D])

    @pl.when(j == 0)
    def _():
        cb_ref[:, 6:8, :] = ctx_ref[...]

    @pl.when(j > 0)
    def _():
        cb_ref[:, 6:8, :] = cb_ref[:, 6 + tt:8 + tt, :]

    cb_ref[:, 8:8 + tt, :] = cu.reshape(nb, tt, D)
    cw = cw_ref[...]
    z = (cb_ref[:, 6:6 + tt, :] * cw[0:1] + cb_ref[:, 7:7 + tt, :] * cw[1:2]).reshape(m, D) + cu * cw[2:3]
    y = _dot((gb * z).astype(bf16), wout_ref[...])
    y_ref[...] = (x + y).reshape(nb, tt, D)
    nctx_ref[...] = cb_ref[:, 6 + tt:8 + tt, :]


def _mixer(x, ctx, g, w_in, cw, w_out, *, nb, tt):
    b, t, _ = x.shape
    grid = (b // nb, t // tt)
    return pl.pallas_call(
        functools.partial(_mix_kernel, nb=nb, tt=tt),
        grid=grid,
        in_specs=[
            pl.BlockSpec((nb, tt, D), lambda i, j: (i, j, 0)),
            pl.BlockSpec((nb, 2, D), lambda i, j: (i, 0, 0)),
            _const_spec((1, D)),
            _const_spec((D, 3 * D)),
            _const_spec((3, D)),
            _const_spec((D, D)),
        ],
        out_specs=[
            pl.BlockSpec((nb, tt, D), lambda i, j: (i, j, 0)),
            pl.BlockSpec((nb, 2, D), lambda i, j: (i, 0, 0)),
        ],
        out_shape=[jax.ShapeDtypeStruct((b, t, D), f32), jax.ShapeDtypeStruct((b, 2, D), f32)],
        scratch_shapes=[pltpu.VMEM((nb, 8 + tt, D), f32)],
        compiler_params=_params(("parallel", "arbitrary")),
        name="mixer_a",
    )(x, ctx, g, w_in, cw, w_out)


def _ffn_kernel(*refs, nb, tt, with_attn, with_final):
    refs = list(refs)
    x_ref = refs.pop(0)
    if with_attn:
        o_ref, wo_ref = refs.pop(0), refs.pop(0)
    ctx_ref, g_ref, wup_ref, cw_ref, wdown_ref = (refs.pop(0) for _ in range(5))
    if with_final:
        fg_ref = refs.pop(0)
    y_ref, nctx_ref, acc_ref, st_ref, cb_ref = refs

    j = pl.program_id(1)
    m = nb * tt
    x = x_ref[...].reshape(m, D)
    if with_attn:
        x = x + _dot(o_ref[...].reshape(m, D), wo_ref[...])
    h = _rms(x, g_ref[...]).astype(bf16)

    @pl.when(j == 0)
    def _():
        st_ref[:, 0:2, :] = ctx_ref[...]

    acc_ref[...] = x

    def conv(u, lo, slot):
        hi = lo + CH
        cb_ref[slot, :, 6:8, :] = st_ref[:, 0:2, lo:hi]
        cb_ref[slot, :, 8:8 + tt, :] = u.reshape(nb, tt, CH)
        cw = cw_ref[:, lo:hi]
        y = (cb_ref[slot, :, 6:6 + tt, :] * cw[0:1] + cb_ref[slot, :, 7:7 + tt, :] * cw[1:2]).reshape(m, CH)
        st_ref[:, 0:2, lo:hi] = cb_ref[slot, :, 6 + tt:8 + tt, :]
        return y + u * cw[2:3]

    for c in range(NCH):
        glo, vlo = c * CH, F + c * CH
        gate = conv(_dot(h, wup_ref[:, glo:glo + CH]), glo, 0)
        val = conv(_dot(h, wup_ref[:, vlo:vlo + CH]), vlo, 1)
        act = (gate * (1.0 / (1.0 + jnp.exp(-gate))) * val).astype(bf16)
        acc_ref[...] += _dot(act, wdown_ref[glo:glo + CH, :])

    y = acc_ref[...]
    if with_final:
        y = _rms(y, fg_ref[...])
    y_ref[...] = y.reshape(nb, tt, D)
    nctx_ref[...] = st_ref[:, 0:2, :]


def _ffn(x, ctx, g, w_up, cw, w_down, *, nb, tt, attn=None, final_g=None):
    b, t, _ = x.shape
    grid = (b // nb, t // tt)
    tile = pl.BlockSpec((nb, tt, D), lambda i, j: (i, j, 0))
    args, specs = [x], [tile]
    if attn is not None:
        o, w_o = attn
        args += [o, w_o]
        specs += [tile, _const_spec((D, D))]
    args += [ctx, g, w_up, cw, w_down]
    specs += [
        pl.BlockSpec((nb, 2, 2 * F), lambda i, j: (i, 0, 0)),
        _const_spec((1, D)),
        _const_spec((D, 2 * F)),
        _const_spec((3, 2 * F)),
        _const_spec((F, D)),
    ]
    if final_g is not None:
        args.append(final_g)
        specs.append(_const_spec((1, D)))
    return pl.pallas_call(
        functools.partial(_ffn_kernel, nb=nb, tt=tt, with_attn=attn is not None, with_final=final_g is not None),
        grid=grid,
        in_specs=specs,
        out_specs=[tile, pl.BlockSpec((nb, 2, 2 * F), lambda i, j: (i, 0, 0))],
        out_shape=[jax.ShapeDtypeStruct((b, t, D), f32), jax.ShapeDtypeStruct((b, 2, 2 * F), f32)],
        scratch_shapes=[
            pltpu.VMEM((nb * tt, D), f32),
            pltpu.VMEM((nb, 8, 2 * F), f32),
            pltpu.VMEM((2, nb, 8 + tt, CH), f32),
        ],
        compiler_params=_params(("parallel", "arbitrary")),
        name="conv_ffn_attn" if attn is not None else "conv_ffn",
    )(*args)


def _split3(c):
    hi = c.astype(bf16).astype(f32)
    r = c - hi
    mid = r.astype(bf16).astype(f32)
    lo = (r - mid).astype(bf16).astype(f32)
    return hi, mid, lo


def _pack3(c):
    hi, mid, lo = _split3(c)
    return hi + pltpu.roll(mid, H, 1) + pltpu.roll(lo, 2 * H, 1)


def _cumsum_cols(logf, carry, tri, eq_ref, ek_ref, oq_ref, ok_ref):
    lane = lax.broadcasted_iota(jnp.int32, logf.shape, 1)
    head = lane < H
    lf = jnp.where(head, logf, 0.0)
    cs = _dot(tri, _pack3(lf).astype(bf16))
    c = cs + pltpu.roll(cs, LANES - H, 1) + pltpu.roll(cs, LANES - 2 * H, 1)
    c = jnp.where(head, c, 0.0) + carry
    cp = _pack3(c * LOG2E).astype(bf16)
    qa = _dot(cp, eq_ref[...]) + oq_ref[...]
    ka = _dot(cp, ek_ref[...]) + ok_ref[...]
    return c, qa.astype(bf16), ka.astype(bf16)


def _tri(nb, tt):
    m = nb * tt
    r = lax.broadcasted_iota(jnp.int32, (m, m), 0)
    c = lax.broadcasted_iota(jnp.int32, (m, m), 1)
    if nb == 1:
        keep = c <= r
    else:
        assert tt & (tt - 1) == 0, "several streams per tile need a power-of-two tile length"
        sh = tt.bit_length() - 1
        keep = (c <= r) & ((r >> sh) == (c >> sh))
    return jnp.where(keep, 1.0, 0.0).astype(bf16)


def _aug_constants():
    eq = np.zeros((LANES, D), np.float32)
    ek = np.zeros((LANES, D), np.float32)
    oq = np.zeros((1, D), np.float32)
    ok = np.zeros((1, D), np.float32)
    for h in range(H):
        base = (h // 2) * LANES + (h % 2) * 2 * PIECES
        for x in range(PIECES):
            eq[x * H + h, base + x] = 1.0
            ok[0, base + x] = 1.0
            oq[0, base + PIECES + x] = 1.0
            ek[x * H + h, base + PIECES + x] = -1.0
    return jnp.asarray(eq, bf16), jnp.asarray(ek, bf16), jnp.asarray(oq), jnp.asarray(ok)


def _proj_kernel(x_ref, c0_ref, kvg_ref, qg_ref, wk_ref, wv_ref, wf_ref, bf_ref, wq_ref,
                 eq_ref, ek_ref, oq_ref, ok_ref,
                 k_ref, v_ref, lf_ref, qm_ref, qa_ref, km_ref, ka_ref, vm_ref, carry_ref, *, nb, tt, v_transposed):
    j = pl.program_id(1)
    m = nb * tt
    x = x_ref[...].reshape(m, D)
    xn = x * lax.rsqrt(jnp.mean(x * x, axis=-1, keepdims=True) + EPS)
    hk = (xn * kvg_ref[...]).astype(bf16)
    hq = (xn * qg_ref[...]).astype(bf16)

    k = _dot(hk, wk_ref[...])
    v = _dot(hk, wv_ref[...])
    k_ref[...] = k.reshape(nb, tt, D)
    v_ref[...] = v.reshape(nb, tt, D)
    km_ref[...] = k.astype(bf16).reshape(nb, tt, D)
    if v_transposed:
        vm_ref[0] = v.T.astype(bf16)
    else:
        vm_ref[...] = v.astype(bf16).reshape(nb, tt, D)
    qm_ref[...] = _dot(hq, wq_ref[...]).astype(bf16).reshape(nb, tt, D)

    logf = jax.nn.log_sigmoid(_dot(hk, wf_ref[...]) + bf_ref[...])
    lf_ref[...] = logf[:, 0:H].reshape(nb, tt, H)

    @pl.when(j == 0)
    def _():
        carry_ref[...] = c0_ref[...]

    carry = jnp.broadcast_to(carry_ref[...], (nb, tt, LANES)).reshape(m, LANES)
    c, qa, ka = _cumsum_cols(logf, carry, _tri(nb, tt), eq_ref, ek_ref, oq_ref, ok_ref)
    qa_ref[...] = qa.reshape(nb, tt, D)
    ka_ref[...] = ka.reshape(nb, tt, D)
    carry_ref[...] = c.reshape(nb, tt, LANES)[:, tt - 1:tt, :]


def _proj(x, c0, kvg, qg, wk, wv, wf, bfp, wq, consts, *, nb, tt, v_transposed):
    b, t, _ = x.shape
    grid = (b // nb, t // tt)
    tile = pl.BlockSpec((nb, tt, D), lambda i, j: (i, j, 0))
    wide = jax.ShapeDtypeStruct((b, t, D), f32)
    half = jax.ShapeDtypeStruct((b, t, D), bf16)
    if v_transposed:
        assert nb == 1
        vm_spec = pl.BlockSpec((1, D, tt), lambda i, j: (i, 0, j))
        vm_shape = jax.ShapeDtypeStruct((b, D, t), bf16)
    else:
        vm_spec, vm_shape = tile, half
    return pl.pallas_call(
        functools.partial(_proj_kernel, nb=nb, tt=tt, v_transposed=v_transposed),
        grid=grid,
        in_specs=[
            tile,
            pl.BlockSpec((nb, 1, LANES), lambda i, j: (i, 0, 0)),
            _const_spec((1, D)), _const_spec((1, D)),
            _const_spec((D, D)), _const_spec((D, D)), _const_spec((D, LANES)), _const_spec((1, LANES)),
            _const_spec((D, D)),
            _const_spec((LANES, D)), _const_spec((LANES, D)), _const_spec((1, D)), _const_spec((1, D)),
        ],
        out_specs=[tile, tile, pl.BlockSpec((nb, tt, H), lambda i, j: (i, j, 0)), tile, tile, tile, tile, vm_spec],
        out_shape=[wide, wide, jax.ShapeDtypeStruct((b, t, H), f32), half, half, half, half, vm_shape],
        scratch_shapes=[pltpu.VMEM((nb, 1, LANES), f32)],
        compiler_params=_params(("parallel", "arbitrary")),
        name="kvq_proj",
    )(x, c0, kvg, qg, wk, wv, wf, bfp, wq, *consts)


def _cache_aug_kernel(lf_ref, place_ref, eq_ref, ek_ref, oq_ref, ok_ref, ka_ref, ctot_ref, *, p):
    lf16 = lf_ref[0]
    hi, mid, lo = _split3(lf16)
    place = place_ref[...]
    logf = _dot(hi.astype(bf16), place) + _dot(mid.astype(bf16), place) + _dot(lo.astype(bf16), place)
    c, _, ka = _cumsum_cols(logf, jnp.zeros((1, LANES), f32), _tri(1, p), eq_ref, ek_ref, oq_ref, ok_ref)
    ka_ref[0] = ka
    ctot_ref[0] = c[p - 1:p, :]


def _cache_aug(cache_logf, consts):
    b, p, _ = cache_logf.shape
    place = jnp.asarray(np.eye(H, LANES, dtype=np.float32), bf16)
    return pl.pallas_call(
        functools.partial(_cache_aug_kernel, p=p),
        grid=(b,),
        in_specs=[
            pl.BlockSpec((1, p, H), lambda i: (i, 0, 0)),
            _const_spec((H, LANES)),
            _const_spec((LANES, D)), _const_spec((LANES, D)), _const_spec((1, D)), _const_spec((1, D)),
        ],
        out_specs=[pl.BlockSpec((1, p, D), lambda i: (i, 0, 0)), pl.BlockSpec((1, 1, LANES), lambda i: (i, 0, 0))],
        out_shape=[jax.ShapeDtypeStruct((b, p, D), bf16), jax.ShapeDtypeStruct((b, 1, LANES), f32)],
        compiler_params=_params(("parallel",)),
        name="cache_aug",
    )(cache_logf, place, *consts)


def _stack_heads(qm, qa):
    t = qm.shape[0]
    q2 = jnp.concatenate([qm, qa], axis=1).astype(f32)
    lane = lax.broadcasted_iota(jnp.int32, (t, 2 * LANES), 1)
    grp = 2 * PIECES
    keep_a = (lane < DH) | ((lane >= LANES) & (lane < LANES + grp))
    keep_b = ((lane >= DH) & (lane < LANES)) | ((lane >= LANES + grp) & (lane < LANES + 2 * grp))
    return jnp.concatenate([jnp.where(keep_a, q2, 0.0), jnp.where(keep_b, q2, 0.0)], axis=0).astype(bf16)


def _unstack_heads(o, t):
    lane = lax.broadcasted_iota(jnp.int32, (t, LANES), 1)
    return jnp.where(lane < DH, o[0:t], o[t:2 * t])


def _causal(s, t):
    row = lax.broadcasted_iota(jnp.int32, s.shape, 0)
    col = lax.broadcasted_iota(jnp.int32, s.shape, 1)
    row = jnp.where(row >= t, row - t, row)
    return jnp.where(col <= row, s, NEG)


QGRP = 256


AHEAD = 2


def _attn_prompt_kernel(qm_ref, qa_ref, km_ref, ka_ref, vt_ref, o_ref,
                        qs_ref, m_ref, l_ref, acc_ref, s_ref, p_ref, a_ref, *, tq):
    i = pl.program_id(2)
    ngrp = 2 * tq // QGRP
    qs_ref[...] = _stack_heads(qm_ref[0], qa_ref[0])
    m_ref[...] = jnp.full(m_ref.shape, NEG, f32)
    l_ref[...] = jnp.zeros(l_ref.shape, f32)
    acc_ref[...] = jnp.zeros(acc_ref.shape, f32)
    p_ref[...] = jnp.zeros(p_ref.shape, bf16)
    a_ref[...] = jnp.ones(a_ref.shape, f32)

    def keys(jb):
        off = pl.multiple_of(jb * tq, tq)
        return jnp.concatenate([km_ref[0, pl.ds(off, tq), :], ka_ref[0, pl.ds(off, tq), :]], axis=1)

    def values(jb):
        vt = vt_ref[0, :, pl.ds(pl.multiple_of(jb * tq, tq), tq)]
        ones = jnp.ones((ONES_ROWS, tq), bf16)
        return [jnp.concatenate([vt[h * DH:(h + 1) * DH, :], ones], axis=0) for h in range(2)]

    def scores(kblk, g):
        return _dot_nt(kblk, qs_ref[g * QGRP:(g + 1) * QGRP, :])

    def weighted_values(vt_aug, p, alpha, g):
        c0 = g * QGRP
        head, q0 = c0 // tq, c0 % tq
        rows = slice(head * DH, (head + 1) * DH)
        r = _dot(vt_aug[head], p)
        acc_ref[rows, q0:q0 + QGRP] = alpha * acc_ref[rows, q0:q0 + QGRP] + r[0:DH]
        l_ref[:, c0:c0 + QGRP] = alpha * l_ref[:, c0:c0 + QGRP] + r[DH:DH + 1]

    def softmax(s, g, masked):
        c0 = g * QGRP
        if masked:
            key = lax.broadcasted_iota(jnp.int32, s.shape, 0)
            qry = lax.broadcasted_iota(jnp.int32, s.shape, 1) + c0 % tq
            s = jnp.where(key <= qry, s, NEG)
        m_prev = m_ref[:, c0:c0 + QGRP]
        m_new = jnp.maximum(m_prev, jnp.max(s, axis=0, keepdims=True))
        m_ref[:, c0:c0 + QGRP] = m_new
        return jnp.exp2(s - m_new).astype(bf16), jnp.exp2(m_prev - m_new)

    def step(jb, masked):
        kblk = keys(jb)
        vt_now = values(jb)
        if not masked:
            kblk_next = keys(jb + 1)
        ready = {}
        prev = (values(jnp.maximum(jb - 1, 0)), p_ref[...], a_ref[...], ngrp - 1)
        for g in range(ngrp):
            s = s_ref[g] if g < AHEAD else ready.pop(g)
            if g + AHEAD < ngrp:
                ready[g + AHEAD] = scores(kblk, g + AHEAD)
            elif not masked:
                s_ref[g + AHEAD - ngrp] = scores(kblk_next, g + AHEAD - ngrp)
            weighted_values(*prev)
            p, alpha = softmax(s, g, masked)
            prev = (vt_now, p, alpha, g)
        if masked:
            weighted_values(*prev)
        else:
            p_ref[...] = prev[1]
            a_ref[...] = prev[2]

    def body(jb, carry):
        step(jb, False)
        return carry

    first = keys(0)
    for g in range(AHEAD):
        s_ref[g] = scores(first, g)
    lax.fori_loop(0, i, body, 0)
    step(i, True)
    inv = 1.0 / l_ref[...]
    ot = jnp.concatenate([acc_ref[0:DH, :] * inv[:, 0:tq], acc_ref[DH:2 * DH, :] * inv[:, tq:2 * tq]], axis=0)
    o_ref[0] = ot.T.astype(bf16)


def _attn_prompt(qm, qa, km, ka, vt, *, tq):
    b, t, _ = qm.shape
    assert tq % QGRP == 0 and 2 * tq // QGRP > AHEAD
    qspec = pl.BlockSpec((1, tq, LANES), lambda bi, p, i: (bi, i, p))
    kspec = pl.BlockSpec((1, t, LANES), lambda bi, p, i: (bi, 0, p))
    vspec = pl.BlockSpec((1, LANES, t), lambda bi, p, i: (bi, p, 0))
    return pl.pallas_call(
        functools.partial(_attn_prompt_kernel, tq=tq),
        grid=(b, NPAIR, t // tq),
        in_specs=[qspec, qspec, kspec, kspec, vspec],
        out_specs=qspec,
        out_shape=jax.ShapeDtypeStruct((b, t, D), bf16),
        scratch_shapes=[
            pltpu.VMEM((2 * tq, 2 * LANES), bf16),
            pltpu.VMEM((1, 2 * tq), f32),
            pltpu.VMEM((1, 2 * tq), f32),
            pltpu.VMEM((LANES, tq), f32),
            pltpu.VMEM((AHEAD, tq, QGRP), f32),
            pltpu.VMEM((tq, QGRP), bf16),
            pltpu.VMEM((1, QGRP), f32),
        ],
        compiler_params=_params(("parallel", "parallel", "arbitrary")),
        name="fox_attn_prompt",
    )(qm, qa, km, ka, vt)


def _attn_sample_kernel(qm_ref, qa_ref, kc_ref, kca_ref, vc_ref, kn_ref, kna_ref, vn_ref, o_ref, *, t):
    qs = _stack_heads(qm_ref[0], qa_ref[0])
    kc = jnp.concatenate([kc_ref[0].astype(bf16), kca_ref[0]], axis=1)
    kn = jnp.concatenate([kn_ref[0], kna_ref[0]], axis=1)
    s_c = _dot_nt(qs, kc)
    s_n = _causal(_dot_nt(qs, kn), t)
    m = jnp.maximum(jnp.max(s_c, axis=1, keepdims=True), jnp.max(s_n, axis=1, keepdims=True))
    p_c = jnp.exp2(s_c - m)
    p_n = jnp.exp2(s_n - m)
    l = jnp.sum(p_c, axis=1, keepdims=True) + jnp.sum(p_n, axis=1, keepdims=True)
    o = _dot(p_c.astype(bf16), vc_ref[0].astype(bf16)) + _dot(p_n.astype(bf16), vn_ref[0])
    o_ref[0] = _unstack_heads(o * (1.0 / l), t).astype(bf16)


def _attn_sample(qm, qa, kc, kca, vc, kn, kna, vn):
    b, t, _ = qm.shape
    p = kc.shape[1]
    new = pl.BlockSpec((1, t, LANES), lambda bi, pr: (bi, 0, pr))
    old = pl.BlockSpec((1, p, LANES), lambda bi, pr: (bi, 0, pr))
    return pl.pallas_call(
        functools.partial(_attn_sample_kernel, t=t),
        grid=(b, NPAIR),
        in_specs=[new, new, old, old, old, new, new, new],
        out_specs=new,
        out_shape=jax.ShapeDtypeStruct((b, t, D), bf16),
        compiler_params=_params(("parallel", "parallel")),
        name="fox_attn_sample",
    )(qm, qa, kc, kca, vc, kn, kna, vn)


def _trunk(x, sa, sf, past, w, consts, *, nb, tt, tq):
    b, t, _ = x.shape
    x1, nsa = _mixer(x, sa[0], w["a_g"], w["w_in"], w["a_cw"], w["w_out"], nb=nb, tt=tt)
    x2, nsf0 = _ffn(x1, sf[0], w["f_g"][0], w["w_up"][0], w["f_cw"][0], w["w_down"][0], nb=nb, tt=tt)
    if past is None:
        c0 = jnp.zeros((b, 1, LANES), f32)
    else:
        cache_k, cache_v, cache_logf = past
        kca, c0 = _cache_aug(cache_logf, consts)
    k, v, logf, qm, qa, km, ka, vm = _proj(x2, c0, w["kv_g"], w["q_g"], w["w_k"], w["w_v"], w["w_f"], w["b_f"],
                                           w["w_q"], consts, nb=nb, tt=tt, v_transposed=past is None)
    if past is None:
        o = _attn_prompt(qm, qa, km, ka, vm, tq=tq)
    else:
        p = cache_k.shape[1]
        o = _attn_sample(qm, qa, cache_k.reshape(b, p, D), kca, cache_v.reshape(b, p, D), km, ka, vm)
    y, nsf1 = _ffn(x2, sf[1], w["f_g"][1], w["w_up"][1], w["f_cw"][1], w["w_down"][1], nb=nb, tt=tt,
                   attn=(o, w["w_o"]), final_g=w["final_g"])
    return (y, nsa[None], jnp.stack([nsf0, nsf1]), k.reshape(b, t, H, DH), v.reshape(b, t, H, DH), logf)


def kernel(x_prompt, x_sample, state_conv_a, state_ffn_conv, cache_k, cache_v, cache_logf, a_norm, w_a_in,
           a_conv_w, w_a_out, kv_norm, w_kv, b_f, b_norm, w_q, w_o, ffn_norm, w_ffn_up, ffn_conv_w,
           w_ffn_down, final_norm):
    assert a_norm.shape[0] == 1 and b_norm.shape[0] == 1 and ffn_norm.shape[0] == 2
    w = {
        "a_g": a_norm[0][None], "w_in": w_a_in[0].astype(bf16), "a_cw": a_conv_w[0],
        "w_out": w_a_out[0].astype(bf16),
        "f_g": [ffn_norm[l][None] for l in range(2)],
        "w_up": [w_ffn_up[l].astype(bf16) for l in range(2)],
        "f_cw": [ffn_conv_w[l] for l in range(2)],
        "w_down": [w_ffn_down[l].astype(bf16) for l in range(2)],
        "kv_g": kv_norm[None], "q_g": b_norm[0][None],
        "w_k": w_kv[:, 0:D].astype(bf16), "w_v": w_kv[:, D:2 * D].astype(bf16),
        "w_f": jnp.pad(w_kv[:, 2 * D:], ((0, 0), (0, LANES - H))).astype(bf16),
        "b_f": jnp.pad(b_f, (0, LANES - H))[None],
        "w_q": (w_q[0] * (DH ** -0.5 * LOG2E)).astype(bf16),
        "w_o": w_o[0].astype(bf16), "final_g": final_norm[None],
    }
    consts = _aug_constants()
    bp = x_prompt.shape[0]
    sa0 = jnp.zeros((1, bp, 2, D), f32)
    sf0 = jnp.zeros((2, bp, 2, 2 * F), f32)
    outs_p = _trunk(x_prompt, sa0, sf0, None, w, consts, nb=1, tt=512, tq=512)
    outs_s = _trunk(x_sample, state_conv_a, state_ffn_conv, (cache_k, cache_v, cache_logf), w, consts,
                    nb=8, tt=x_sample.shape[1], tq=None)
    y_p, p_a, p_f, p_k, p_v, p_lf = outs_p
    y_s, s_a, s_f, s_k, s_v, s_lf = outs_s
    return (y_p, y_s, p_a, p_f, p_k, p_v, p_lf, s_a, s_f, s_k, s_v, s_lf)
```

```python
import functools

import numpy as np
import jax
import jax.numpy as jnp
from jax import lax
from jax.experimental import pallas as pl
from jax.experimental.pallas import tpu as pltpu

D = 1024
F = 2816
H = 16
DH = 64
NPAIR = H // 2
LANES = 128
CH = 256
NCH = F // CH
EPS = 1e-6
NEG = -1e30
PIECES = 3
LOG2E = 1.4426950408889634
ONES_ROWS = 16
VMEM_LIMIT = 56 * 1024 * 1024

f32 = jnp.float32
bf16 = jnp.bfloat16


def _const_spec(shape):
    zeros = (0,) * len(shape)
    return pl.BlockSpec(shape, lambda *_: zeros, pipeline_mode=pl.Buffered(1))


def _params(sem):
    return pltpu.CompilerParams(dimension_semantics=sem, vmem_limit_bytes=VMEM_LIMIT)


def _rms(x, g):
    ms = jnp.mean(x * x, axis=-1, keepdims=True)
    return (x * lax.rsqrt(ms + EPS)) * g


def _dot(a, b):
    return jnp.dot(a, b, preferred_element_type=f32)


def _dot_nt(a, b):
    return lax.dot_general(a, b, (((1,), (1,)), ((), ())), preferred_element_type=f32)


def _mix_kernel(x_ref, ctx_ref, g_ref, win_ref, cw_ref, wout_ref, y_ref, nctx_ref, cb_ref, zg_ref, *, nb, tt):
    j = pl.program_id(1)
    m = nb * tt
    x = x_ref[...].reshape(m, D)
    h = _rms(x, g_ref[...]).astype(bf16)

    @pl.when(j == 0)
    def _():
        cb_ref[:, 6:8, :] = ctx_ref[...]

    @pl.when(j > 0)
    def _():
        cb_ref[:, 6:8, :] = cb_ref[:, 6 + tt:8 + tt, :]

    def up(c):
        lo = c * CH
        return tuple(_dot(h, win_ref[:, part * D + lo:part * D + lo + CH]) for part in range(3))

    nch = D // CH
    y = x
    nxt = up(0)
    for c in range(nch):
        lo = c * CH
        gb, gc, u = nxt
        if c + 1 < nch:
            nxt = up(c + 1)
        cu = gc * u
        cb_ref[:, 8:8 + tt, lo:lo + CH] = cu.reshape(nb, tt, CH)
        cw = cw_ref[:, lo:lo + CH]
        z = (cb_ref[:, 6:6 + tt, lo:lo + CH] * cw[0:1]
             + cb_ref[:, 7:7 + tt, lo:lo + CH] * cw[1:2]).reshape(m, CH) + cu * cw[2:3]
        zg_ref[:, lo:lo + CH] = (gb * z).astype(bf16)
        if c == nch // 2 - 1:
            y = y + _dot(zg_ref[:, 0:lo + CH], wout_ref[0:lo + CH, :])
    y = y + _dot(zg_ref[:, D // 2:D], wout_ref[D // 2:D, :])
    y_ref[...] = y.reshape(nb, tt, D)
    nctx_ref[...] = cb_ref[:, 6 + tt:8 + tt, :]


def _mixer(x, ctx, g, w_in, cw, w_out, *, nb, tt):
    b, t, _ = x.shape
    grid = (b // nb, t // tt)
    return pl.pallas_call(
        functools.partial(_mix_kernel, nb=nb, tt=tt),
        grid=grid,
        in_specs=[
            pl.BlockSpec((nb, tt, D), lambda i, j: (i, j, 0)),
            pl.BlockSpec((nb, 2, D), lambda i, j: (i, 0, 0)),
            _const_spec((1, D)),
            _const_spec((D, 3 * D)),
            _const_spec((3, D)),
            _const_spec((D, D)),
        ],
        out_specs=[
            pl.BlockSpec((nb, tt, D), lambda i, j: (i, j, 0)),
            pl.BlockSpec((nb, 2, D), lambda i, j: (i, 0, 0)),
        ],
        out_shape=[jax.ShapeDtypeStruct((b, t, D), f32), jax.ShapeDtypeStruct((b, 2, D), f32)],
        scratch_shapes=[pltpu.VMEM((nb, 8 + tt, D), f32), pltpu.VMEM((nb * tt, D), bf16)],
        compiler_params=_params(("parallel", "arbitrary")),
        name="mixer_a",
    )(x, ctx, g, w_in, cw, w_out)


def _ffn_kernel(*refs, nb, tt, with_attn, with_final):
    refs = list(refs)
    x_ref = refs.pop(0)
    if with_attn:
        o_ref, wo_ref = refs.pop(0), refs.pop(0)
    ctx_ref, g_ref, wup_ref, cw_ref, wdown_ref = (refs.pop(0) for _ in range(5))
    if with_final:
        fg_ref = refs.pop(0)
    y_ref, nctx_ref, act_ref, st_ref, cb_ref = refs

    j = pl.program_id(1)
    m = nb * tt
    x = x_ref[...].reshape(m, D)
    if with_attn:
        x = x + _dot(o_ref[...].reshape(m, D), wo_ref[...])
    h = _rms(x, g_ref[...]).astype(bf16)

    @pl.when(j == 0)
    def _():
        st_ref[:, 0:2, :] = ctx_ref[...]

    def up(c):
        glo, vlo = c * CH, F + c * CH
        return _dot(h, wup_ref[:, glo:glo + CH]), _dot(h, wup_ref[:, vlo:vlo + CH])

    def conv(u, lo, slot):
        hi = lo + CH
        cb_ref[slot, :, 6:8, :] = st_ref[:, 0:2, lo:hi]
        cb_ref[slot, :, 8:8 + tt, :] = u.reshape(nb, tt, CH)
        cw = cw_ref[:, lo:hi]
        y = (cb_ref[slot, :, 6:6 + tt, :] * cw[0:1] + cb_ref[slot, :, 7:7 + tt, :] * cw[1:2]).reshape(m, CH)
        st_ref[:, 0:2, lo:hi] = cb_ref[slot, :, 6 + tt:8 + tt, :]
        return y + u * cw[2:3]

    split = (NCH + 1) // 2 * CH
    y = x
    nxt = up(0)
    for c in range(NCH):
        glo, vlo = c * CH, F + c * CH
        ug, uv = nxt
        if c + 1 < NCH:
            nxt = up(c + 1)
        gate = conv(ug, glo, 0)
        val = conv(uv, vlo, 1)
        act_ref[:, glo:glo + CH] = (gate * (1.0 / (1.0 + jnp.exp(-gate))) * val).astype(bf16)
        if glo + CH == split:
            y = y + _dot(act_ref[:, 0:split], wdown_ref[0:split, :])
    y = y + _dot(act_ref[:, split:F], wdown_ref[split:F, :])
    if with_final:
        y = _rms(y, fg_ref[...])
    y_ref[...] = y.reshape(nb, tt, D)
    nctx_ref[...] = st_ref[:, 0:2, :]


def _ffn(x, ctx, g, w_up, cw, w_down, *, nb, tt, attn=None, final_g=None):
    b, t, _ = x.shape
    grid = (b // nb, t // tt)
    tile = pl.BlockSpec((nb, tt, D), lambda i, j: (i, j, 0))
    args, specs = [x], [tile]
    if attn is not None:
        o, w_o = attn
        args += [o, w_o]
        specs += [tile, _const_spec((D, D))]
    args += [ctx, g, w_up, cw, w_down]
    specs += [
        pl.BlockSpec((nb, 2, 2 * F), lambda i, j: (i, 0, 0)),
        _const_spec((1, D)),
        _const_spec((D, 2 * F)),
        _const_spec((3, 2 * F)),
        _const_spec((F, D)),
    ]
    if final_g is not None:
        args.append(final_g)
        specs.append(_const_spec((1, D)))
    return pl.pallas_call(
        functools.partial(_ffn_kernel, nb=nb, tt=tt, with_attn=attn is not None, with_final=final_g is not None),
        grid=grid,
        in_specs=specs,
        out_specs=[tile, pl.BlockSpec((nb, 2, 2 * F), lambda i, j: (i, 0, 0))],
        out_shape=[jax.ShapeDtypeStruct((b, t, D), f32), jax.ShapeDtypeStruct((b, 2, 2 * F), f32)],
        scratch_shapes=[
            pltpu.VMEM((nb * tt, F), bf16),
            pltpu.VMEM((nb, 8, 2 * F), f32),
            pltpu.VMEM((2, nb, 8 + tt, CH), f32),
        ],
        compiler_params=_params(("parallel", "arbitrary")),
        name="conv_ffn_attn" if attn is not None else "conv_ffn",
    )(*args)


def _split3(c):
    hi = c.astype(bf16).astype(f32)
    r = c - hi
    mid = r.astype(bf16).astype(f32)
    lo = (r - mid).astype(bf16).astype(f32)
    return hi, mid, lo


def _pack3(c):
    hi, mid, lo = _split3(c)
    return hi + pltpu.roll(mid, H, 1) + pltpu.roll(lo, 2 * H, 1)


def _cumsum_cols(logf, carry, tri, eq_ref, ek_ref, oq_ref, ok_ref):
    lane = lax.broadcasted_iota(jnp.int32, logf.shape, 1)
    head = lane < H
    lf = jnp.where(head, logf, 0.0)
    cs = _dot(tri, _pack3(lf).astype(bf16))
    c = cs + pltpu.roll(cs, LANES - H, 1) + pltpu.roll(cs, LANES - 2 * H, 1)
    c = jnp.where(head, c, 0.0) + carry
    cp = _pack3(c * LOG2E).astype(bf16)
    qa = _dot(cp, eq_ref[...]) + oq_ref[...]
    ka = _dot(cp, ek_ref[...]) + ok_ref[...]
    return c, qa.astype(bf16), ka.astype(bf16)


def _tri(nb, tt):
    m = nb * tt
    r = lax.broadcasted_iota(jnp.int32, (m, m), 0)
    c = lax.broadcasted_iota(jnp.int32, (m, m), 1)
    if nb == 1:
        keep = c <= r
    else:
        assert tt & (tt - 1) == 0, "several streams per tile need a power-of-two tile length"
        sh = tt.bit_length() - 1
        keep = (c <= r) & ((r >> sh) == (c >> sh))
    return jnp.where(keep, 1.0, 0.0).astype(bf16)


def _aug_constants():
    eq = np.zeros((LANES, D), np.float32)
    ek = np.zeros((LANES, D), np.float32)
    oq = np.zeros((1, D), np.float32)
    ok = np.zeros((1, D), np.float32)
    for h in range(H):
        base = (h // 2) * LANES + (h % 2) * 2 * PIECES
        for x in range(PIECES):
            eq[x * H + h, base + x] = 1.0
            ok[0, base + x] = 1.0
            oq[0, base + PIECES + x] = 1.0
            ek[x * H + h, base + PIECES + x] = -1.0
    return jnp.asarray(eq, bf16), jnp.asarray(ek, bf16), jnp.asarray(oq), jnp.asarray(ok)


def _proj_kernel(x_ref, c0_ref, kvg_ref, qg_ref, wk_ref, wv_ref, wf_ref, bf_ref, wq_ref,
                 eq_ref, ek_ref, oq_ref, ok_ref,
                 k_ref, v_ref, lf_ref, qm_ref, qa_ref, km_ref, ka_ref, vm_ref, carry_ref, *, nb, tt, v_transposed):
    j = pl.program_id(1)
    m = nb * tt
    x = x_ref[...].reshape(m, D)
    xn = x * lax.rsqrt(jnp.mean(x * x, axis=-1, keepdims=True) + EPS)
    hk = (xn * kvg_ref[...]).astype(bf16)
    hq = (xn * qg_ref[...]).astype(bf16)

    zf = _dot(hk, wf_ref[...])
    k = _dot(hk, wk_ref[...])
    v = _dot(hk, wv_ref[...])
    q = _dot(hq, wq_ref[...])
    k_ref[...] = k.reshape(nb, tt, H, DH)
    v_ref[...] = v.reshape(nb, tt, H, DH)
    km_ref[...] = k.astype(bf16).reshape(nb, tt, D)
    if v_transposed:
        vm_ref[0] = v.T.astype(bf16)
    else:
        vm_ref[...] = v.astype(bf16).reshape(nb, tt, D)
    qm_ref[...] = q.astype(bf16).reshape(nb, tt, D)

    logf = jax.nn.log_sigmoid(zf + bf_ref[...])
    lf_ref[...] = logf[:, 0:H].reshape(nb, tt, H)

    @pl.when(j == 0)
    def _():
        carry_ref[...] = c0_ref[...]

    carry = jnp.broadcast_to(carry_ref[...], (nb, tt, LANES)).reshape(m, LANES)
    c, qa, ka = _cumsum_cols(logf, carry, _tri(nb, tt), eq_ref, ek_ref, oq_ref, ok_ref)
    qa_ref[...] = qa.reshape(nb, tt, D)
    ka_ref[...] = ka.reshape(nb, tt, D)
    carry_ref[...] = c.reshape(nb, tt, LANES)[:, tt - 1:tt, :]


def _proj(x, c0, kvg, qg, wk, wv, wf, bfp, wq, consts, *, nb, tt, v_transposed):
    b, t, _ = x.shape
    grid = (b // nb, t // tt)
    tile = pl.BlockSpec((nb, tt, D), lambda i, j: (i, j, 0))
    heads = pl.BlockSpec((nb, tt, H, DH), lambda i, j: (i, j, 0, 0))
    wide = jax.ShapeDtypeStruct((b, t, H, DH), f32)
    half = jax.ShapeDtypeStruct((b, t, D), bf16)
    if v_transposed:
        assert nb == 1
        vm_spec = pl.BlockSpec((1, D, tt), lambda i, j: (i, 0, j))
        vm_shape = jax.ShapeDtypeStruct((b, D, t), bf16)
    else:
        vm_spec, vm_shape = tile, half
    return pl.pallas_call(
        functools.partial(_proj_kernel, nb=nb, tt=tt, v_transposed=v_transposed),
        grid=grid,
        in_specs=[
            tile,
            pl.BlockSpec((nb, 1, LANES), lambda i, j: (i, 0, 0)),
            _const_spec((1, D)), _const_spec((1, D)),
            _const_spec((D, D)), _const_spec((D, D)), _const_spec((D, LANES)), _const_spec((1, LANES)),
            _const_spec((D, D)),
            _const_spec((LANES, D)), _const_spec((LANES, D)), _const_spec((1, D)), _const_spec((1, D)),
        ],
        out_specs=[heads, heads, pl.BlockSpec((nb, tt, H), lambda i, j: (i, j, 0)), tile, tile, tile, tile, vm_spec],
        out_shape=[wide, wide, jax.ShapeDtypeStruct((b, t, H), f32), half, half, half, half, vm_shape],
        scratch_shapes=[pltpu.VMEM((nb, 1, LANES), f32)],
        compiler_params=_params(("parallel", "arbitrary")),
        name="kvq_proj",
    )(x, c0, kvg, qg, wk, wv, wf, bfp, wq, *consts)


def _cache_aug_kernel(lf_ref, place_ref, eq_ref, ek_ref, oq_ref, ok_ref, ka_ref, ctot_ref, *, p):
    lf16 = lf_ref[0]
    hi, mid, lo = _split3(lf16)
    place = place_ref[...]
    logf = _dot(hi.astype(bf16), place) + _dot(mid.astype(bf16), place) + _dot(lo.astype(bf16), place)
    c, _, ka = _cumsum_cols(logf, jnp.zeros((1, LANES), f32), _tri(1, p), eq_ref, ek_ref, oq_ref, ok_ref)
    ka_ref[0] = ka
    ctot_ref[0] = c[p - 1:p, :]


def _cache_aug(cache_logf, consts):
    b, p, _ = cache_logf.shape
    place = jnp.asarray(np.eye(H, LANES, dtype=np.float32), bf16)
    return pl.pallas_call(
        functools.partial(_cache_aug_kernel, p=p),
        grid=(b,),
        in_specs=[
            pl.BlockSpec((1, p, H), lambda i: (i, 0, 0)),
            _const_spec((H, LANES)),
            _const_spec((LANES, D)), _const_spec((LANES, D)), _const_spec((1, D)), _const_spec((1, D)),
        ],
        out_specs=[pl.BlockSpec((1, p, D), lambda i: (i, 0, 0)), pl.BlockSpec((1, 1, LANES), lambda i: (i, 0, 0))],
        out_shape=[jax.ShapeDtypeStruct((b, p, D), bf16), jax.ShapeDtypeStruct((b, 1, LANES), f32)],
        compiler_params=_params(("parallel",)),
        name="cache_aug",
    )(cache_logf, place, *consts)


def _stack_heads(qm, qa):
    t = qm.shape[0]
    q2 = jnp.concatenate([qm, qa], axis=1).astype(f32)
    lane = lax.broadcasted_iota(jnp.int32, (t, 2 * LANES), 1)
    grp = 2 * PIECES
    keep_a = (lane < DH) | ((lane >= LANES) & (lane < LANES + grp))
    keep_b = ((lane >= DH) & (lane < LANES)) | ((lane >= LANES + grp) & (lane < LANES + 2 * grp))
    return jnp.concatenate([jnp.where(keep_a, q2, 0.0), jnp.where(keep_b, q2, 0.0)], axis=0).astype(bf16)


def _unstack_heads(o, t):
    lane = lax.broadcasted_iota(jnp.int32, (t, LANES), 1)
    return jnp.where(lane < DH, o[0:t], o[t:2 * t])


def _causal(s, t):
    row = lax.broadcasted_iota(jnp.int32, s.shape, 0)
    col = lax.broadcasted_iota(jnp.int32, s.shape, 1)
    row = jnp.where(row >= t, row - t, row)
    return jnp.where(col <= row, s, NEG)


QGRP = 256


AHEAD = 2


def _attn_prompt_kernel(qm_ref, qa_ref, km_ref, ka_ref, vt_ref, o_ref,
                        qs_ref, m_ref, l_ref, acc_ref, s_ref, p_ref, a_ref, *, tq):
    i = pl.program_id(2)
    ngrp = 2 * tq // QGRP
    qs_ref[...] = _stack_heads(qm_ref[0], qa_ref[0])
    m_ref[...] = jnp.full(m_ref.shape, NEG, f32)
    l_ref[...] = jnp.zeros(l_ref.shape, f32)
    acc_ref[...] = jnp.zeros(acc_ref.shape, f32)
    p_ref[...] = jnp.zeros(p_ref.shape, bf16)
    a_ref[...] = jnp.ones(a_ref.shape, f32)

    def keys(jb):
        off = pl.multiple_of(jb * tq, tq)
        return jnp.concatenate([km_ref[0, pl.ds(off, tq), :], ka_ref[0, pl.ds(off, tq), :]], axis=1)

    def values(jb):
        vt = vt_ref[0, :, pl.ds(pl.multiple_of(jb * tq, tq), tq)]
        ones = jnp.ones((ONES_ROWS, tq), bf16)
        return [jnp.concatenate([vt[h * DH:(h + 1) * DH, :], ones], axis=0) for h in range(2)]

    def scores(kblk, g):
        return _dot_nt(kblk, qs_ref[g * QGRP:(g + 1) * QGRP, :])

    def weighted_values(vt_aug, p, alpha, g):
        c0 = g * QGRP
        head, q0 = c0 // tq, c0 % tq
        rows = slice(head * DH, (head + 1) * DH)
        r = _dot(vt_aug[head], p)
        acc_ref[rows, q0:q0 + QGRP] = alpha * acc_ref[rows, q0:q0 + QGRP] + r[0:DH]
        l_ref[:, c0:c0 + QGRP] = alpha * l_ref[:, c0:c0 + QGRP] + r[DH:DH + 1]

    def softmax(s, g, masked):
        c0 = g * QGRP
        if masked:
            key = lax.broadcasted_iota(jnp.int32, s.shape, 0)
            qry = lax.broadcasted_iota(jnp.int32, s.shape, 1) + c0 % tq
            s = jnp.where(key <= qry, s, NEG)
        m_prev = m_ref[:, c0:c0 + QGRP]
        m_new = jnp.maximum(m_prev, jnp.max(s, axis=0, keepdims=True))
        m_ref[:, c0:c0 + QGRP] = m_new
        return jnp.exp2(s - m_new).astype(bf16), jnp.exp2(m_prev - m_new)

    def run(jbs, masked, nxt):
        kb = [keys(jb) for jb in jbs]
        vb = [values(jb) for jb in jbs]
        kn = None if nxt is None else keys(nxt)
        items = [(bi, g) for bi in range(len(jbs)) for g in range(ngrp)]
        ready = {}
        prev = (values(jnp.maximum(jbs[0] - 1, 0)), p_ref[...], a_ref[...], ngrp - 1)
        for t, (bi, g) in enumerate(items):
            s = s_ref[t] if t < AHEAD else ready.pop(t)
            ta = t + AHEAD
            if ta < len(items):
                ready[ta] = scores(kb[items[ta][0]], items[ta][1])
            elif kn is not None:
                s_ref[ta - len(items)] = scores(kn, ta - len(items))
            weighted_values(*prev)
            p, alpha = softmax(s, g, masked)
            prev = (vb[bi], p, alpha, g)
        if nxt is None:
            weighted_values(*prev)
        else:
            p_ref[...] = prev[1]
            a_ref[...] = prev[2]

    def pair_body(t, carry):
        run([2 * t, 2 * t + 1], False, 2 * t + 2)
        return carry

    first = keys(0)
    for g in range(AHEAD):
        s_ref[g] = scores(first, g)
    lax.fori_loop(0, i >> 1, pair_body, 0)

    @pl.when((i & 1) == 1)
    def _():
        run([i - 1], False, i)

    run([i], True, None)
    inv = 1.0 / l_ref[...]
    ot = jnp.concatenate([acc_ref[0:DH, :] * inv[:, 0:tq], acc_ref[DH:2 * DH, :] * inv[:, tq:2 * tq]], axis=0)
    o_ref[0] = ot.T.astype(bf16)


def _attn_prompt(qm, qa, km, ka, vt, *, tq):
    b, t, _ = qm.shape
    assert tq % QGRP == 0 and 2 * tq // QGRP > AHEAD
    qspec = pl.BlockSpec((1, tq, LANES), lambda bi, p, i: (bi, i, p))
    kspec = pl.BlockSpec((1, t, LANES), lambda bi, p, i: (bi, 0, p))
    vspec = pl.BlockSpec((1, LANES, t), lambda bi, p, i: (bi, p, 0))
    return pl.pallas_call(
        functools.partial(_attn_prompt_kernel, tq=tq),
        grid=(b, NPAIR, t // tq),
        in_specs=[qspec, qspec, kspec, kspec, vspec],
        out_specs=qspec,
        out_shape=jax.ShapeDtypeStruct((b, t, D), bf16),
        scratch_shapes=[
            pltpu.VMEM((2 * tq, 2 * LANES), bf16),
            pltpu.VMEM((1, 2 * tq), f32),
            pltpu.VMEM((1, 2 * tq), f32),
            pltpu.VMEM((LANES, tq), f32),
            pltpu.VMEM((AHEAD, tq, QGRP), f32),
            pltpu.VMEM((tq, QGRP), bf16),
            pltpu.VMEM((1, QGRP), f32),
        ],
        compiler_params=_params(("parallel", "parallel", "arbitrary")),
        name="fox_attn_prompt",
    )(qm, qa, km, ka, vt)


def _attn_sample_kernel(qm_ref, qa_ref, kc_ref, kca_ref, vc_ref, kn_ref, kna_ref, vn_ref, o_ref, *, t):
    qs = _stack_heads(qm_ref[0], qa_ref[0])
    kc = jnp.concatenate([kc_ref[0].astype(bf16), kca_ref[0]], axis=1)
    kn = jnp.concatenate([kn_ref[0], kna_ref[0]], axis=1)
    s_c = _dot_nt(qs, kc)
    s_n = _causal(_dot_nt(qs, kn), t)
    m = jnp.maximum(jnp.max(s_c, axis=1, keepdims=True), jnp.max(s_n, axis=1, keepdims=True))
    p_c = jnp.exp2(s_c - m)
    p_n = jnp.exp2(s_n - m)
    l = jnp.sum(p_c, axis=1, keepdims=True) + jnp.sum(p_n, axis=1, keepdims=True)
    o = _dot(p_c.astype(bf16), vc_ref[0].astype(bf16)) + _dot(p_n.astype(bf16), vn_ref[0])
    o_ref[0] = _unstack_heads(o * (1.0 / l), t).astype(bf16)


def _attn_sample(qm, qa, kc, kca, vc, kn, kna, vn):
    b, t, _ = qm.shape
    p = kc.shape[1]
    new = pl.BlockSpec((1, t, LANES), lambda bi, pr: (bi, 0, pr))
    old = pl.BlockSpec((1, p, LANES), lambda bi, pr: (bi, 0, pr))
    return pl.pallas_call(
        functools.partial(_attn_sample_kernel, t=t),
        grid=(b, NPAIR),
        in_specs=[new, new, old, old, old, new, new, new],
        out_specs=new,
        out_shape=jax.ShapeDtypeStruct((b, t, D), bf16),
        compiler_params=_params(("parallel", "parallel")),
        name="fox_attn_sample",
    )(qm, qa, kc, kca, vc, kn, kna, vn)


def _trunk(x, sa, sf, past, w, consts, *, nb, tt, tq):
    b, t, _ = x.shape
    x1, nsa = _mixer(x, sa[0], w["a_g"], w["w_in"], w["a_cw"], w["w_out"], nb=nb, tt=tt)
    x2, nsf0 = _ffn(x1, sf[0], w["f_g"][0], w["w_up"][0], w["f_cw"][0], w["w_down"][0], nb=nb, tt=tt)
    if past is None:
        c0 = jnp.zeros((b, 1, LANES), f32)
    else:
        cache_k, cache_v, cache_logf = past
        kca, c0 = _cache_aug(cache_logf, consts)
    k, v, logf, qm, qa, km, ka, vm = _proj(x2, c0, w["kv_g"], w["q_g"], w["w_k"], w["w_v"], w["w_f"], w["b_f"],
                                           w["w_q"], consts, nb=nb, tt=tt, v_transposed=past is None)
    if past is None:
        o = _attn_prompt(qm, qa, km, ka, vm, tq=tq)
    else:
        p = cache_k.shape[1]
        o = _attn_sample(qm, qa, cache_k.reshape(b, p, D), kca, cache_v.reshape(b, p, D), km, ka, vm)
    y, nsf1 = _ffn(x2, sf[1], w["f_g"][1], w["w_up"][1], w["f_cw"][1], w["w_down"][1], nb=nb, tt=tt,
                   attn=(o, w["w_o"]), final_g=w["final_g"])
    return (y, nsa[None], jnp.stack([nsf0, nsf1]), k, v, logf)


def kernel(x_prompt, x_sample, state_conv_a, state_ffn_conv, cache_k, cache_v, cache_logf, a_norm, w_a_in,
           a_conv_w, w_a_out, kv_norm, w_kv, b_f, b_norm, w_q, w_o, ffn_norm, w_ffn_up, ffn_conv_w,
           w_ffn_down, final_norm):
    assert a_norm.shape[0] == 1 and b_norm.shape[0] == 1 and ffn_norm.shape[0] == 2
    w = {
        "a_g": a_norm[0][None], "w_in": w_a_in[0].astype(bf16), "a_cw": a_conv_w[0],
        "w_out": w_a_out[0].astype(bf16),
        "f_g": [ffn_norm[l][None] for l in range(2)],
        "w_up": [w_ffn_up[l].astype(bf16) for l in range(2)],
        "f_cw": [ffn_conv_w[l] for l in range(2)],
        "w_down": [w_ffn_down[l].astype(bf16) for l in range(2)],
        "kv_g": kv_norm[None], "q_g": b_norm[0][None],
        "w_k": w_kv[:, 0:D].astype(bf16), "w_v": w_kv[:, D:2 * D].astype(bf16),
        "w_f": jnp.pad(w_kv[:, 2 * D:], ((0, 0), (0, LANES - H))).astype(bf16),
        "b_f": jnp.pad(b_f, (0, LANES - H))[None],
        "w_q": (w_q[0] * (DH ** -0.5 * LOG2E)).astype(bf16),
        "w_o": w_o[0].astype(bf16), "final_g": final_norm[None],
    }
    consts = _aug_constants()
    bp = x_prompt.shape[0]
    sa0 = jnp.zeros((1, bp, 2, D), f32)
    sf0 = jnp.zeros((2, bp, 2, 2 * F), f32)
    outs_p = _trunk(x_prompt, sa0, sf0, None, w, consts, nb=1, tt=512, tq=512)
    outs_s = _trunk(x_sample, state_conv_a, state_ffn_conv, (cache_k, cache_v, cache_logf), w, consts,
                    nb=8, tt=x_sample.shape[1], tq=None)
    y_p, p_a, p_f, p_k, p_v, p_lf = outs_p
    y_s, s_a, s_f, s_k, s_v, s_lf = outs_s
    return (y_p, y_s, p_a, p_f, p_k, p_v, p_lf, s_a, s_f, s_k, s_v, s_lf)
```

```python
import functools

import numpy as np
import jax
import jax.numpy as jnp
from jax import lax
from jax.experimental import pallas as pl
from jax.experimental.pallas import tpu as pltpu

D = 1024
F = 2816
H = 16
DH = 64
NPAIR = H // 2
LANES = 128
CH = 256
NCH = F // CH
EPS = 1e-6
NEG = -1e30
PIECES = 3
LOG2E = 1.4426950408889634
ONES_ROWS = 16
VMEM_LIMIT = 56 * 1024 * 1024

f32 = jnp.float32
bf16 = jnp.bfloat16


def _const_spec(shape):
    zeros = (0,) * len(shape)
    return pl.BlockSpec(shape, lambda *_: zeros, pipeline_mode=pl.Buffered(1))


def _params(sem):
    return pltpu.CompilerParams(dimension_semantics=sem, vmem_limit_bytes=VMEM_LIMIT)


def _rms(x, g):
    ms = jnp.mean(x * x, axis=-1, keepdims=True)
    return (x * lax.rsqrt(ms + EPS)) * g


def _dot(a, b):
    return jnp.dot(a, b, preferred_element_type=f32)


def _dot_nt(a, b):
    return lax.dot_general(a, b, (((1,), (1,)), ((), ())), preferred_element_type=f32)


HIST = 8


def _init_history(st_ref, ctx_ref):
    st_ref[...] = jnp.zeros(st_ref.shape, f32)
    st_ref[:, HIST - 2:HIST, :] = ctx_ref[...]


def _causal_conv(u, st_ref, cw_ref, lo, nb, tt):
    hi = lo + CH
    u3 = u.reshape(nb, tt, CH)
    rows = nb * (HIST + tt)
    xp = jnp.concatenate([st_ref[:, :, lo:hi], u3], axis=1).reshape(rows, CH)
    cw = cw_ref[:, lo:hi]
    y = pltpu.roll(xp, 2, 0) * cw[0:1] + pltpu.roll(xp, 1, 0) * cw[1:2]
    st_ref[:, HIST - 2:HIST, lo:hi] = u3[:, tt - 2:tt, :]
    return y.reshape(nb, HIST + tt, CH)[:, HIST:, :].reshape(nb * tt, CH) + u * cw[2:3]


def _mix_kernel(x_ref, ctx_ref, g_ref, win_ref, cw_ref, wout_ref, y_ref, nctx_ref, st_ref, zg_ref, *, nb, tt):
    j = pl.program_id(1)
    m = nb * tt
    x = x_ref[...].reshape(m, D)
    h = _rms(x, g_ref[...]).astype(bf16)

    @pl.when(j == 0)
    def _():
        _init_history(st_ref, ctx_ref)

    def up(c):
        lo = c * CH
        return tuple(_dot(h, win_ref[:, part * D + lo:part * D + lo + CH]) for part in range(3))

    nch = D // CH
    y = x
    nxt = up(0)
    for c in range(nch):
        lo = c * CH
        gb, gc, u = nxt
        if c + 1 < nch:
            nxt = up(c + 1)
        z = _causal_conv(gc * u, st_ref, cw_ref, lo, nb, tt)
        zg_ref[:, lo:lo + CH] = (gb * z).astype(bf16)
        if c == nch // 2 - 1:
            y = y + _dot(zg_ref[:, 0:lo + CH], wout_ref[0:lo + CH, :])
    y = y + _dot(zg_ref[:, D // 2:D], wout_ref[D // 2:D, :])
    y_ref[...] = y.reshape(nb, tt, D)
    nctx_ref[...] = st_ref[:, HIST - 2:HIST, :]


def _mixer(x, ctx, g, w_in, cw, w_out, *, nb, tt):
    b, t, _ = x.shape
    grid = (b // nb, t // tt)
    return pl.pallas_call(
        functools.partial(_mix_kernel, nb=nb, tt=tt),
        grid=grid,
        in_specs=[
            pl.BlockSpec((nb, tt, D), lambda i, j: (i, j, 0)),
            pl.BlockSpec((nb, 2, D), lambda i, j: (i, 0, 0)),
            _const_spec((1, D)),
            _const_spec((D, 3 * D)),
            _const_spec((3, D)),
            _const_spec((D, D)),
        ],
        out_specs=[
            pl.BlockSpec((nb, tt, D), lambda i, j: (i, j, 0)),
            pl.BlockSpec((nb, 2, D), lambda i, j: (i, 0, 0)),
        ],
        out_shape=[jax.ShapeDtypeStruct((b, t, D), f32), jax.ShapeDtypeStruct((b, 2, D), f32)],
        scratch_shapes=[pltpu.VMEM((nb, HIST, D), f32), pltpu.VMEM((nb * tt, D), bf16)],
        compiler_params=_params(("parallel", "arbitrary")),
        name="mixer_a",
    )(x, ctx, g, w_in, cw, w_out)


def _ffn_kernel(*refs, nb, tt, with_attn, with_final):
    refs = list(refs)
    x_ref = refs.pop(0)
    if with_attn:
        o_ref, wo_ref = refs.pop(0), refs.pop(0)
    ctx_ref, g_ref, wup_ref, cw_ref, wdown_ref = (refs.pop(0) for _ in range(5))
    if with_final:
        fg_ref = refs.pop(0)
    y_ref, nctx_ref, act_ref, st_ref = refs

    j = pl.program_id(1)
    m = nb * tt
    x = x_ref[...].reshape(m, D)
    if with_attn:
        x = x + _dot(o_ref[...].reshape(m, D), wo_ref[...])
    h = _rms(x, g_ref[...]).astype(bf16)

    @pl.when(j == 0)
    def _():
        _init_history(st_ref, ctx_ref)

    def up(c):
        glo, vlo = c * CH, F + c * CH
        return _dot(h, wup_ref[:, glo:glo + CH]), _dot(h, wup_ref[:, vlo:vlo + CH])

    split = (NCH + 1) // 2 * CH
    y = x
    nxt = up(0)
    for c in range(NCH):
        glo, vlo = c * CH, F + c * CH
        ug, uv = nxt
        if c + 1 < NCH:
            nxt = up(c + 1)
        gate = _causal_conv(ug, st_ref, cw_ref, glo, nb, tt)
        val = _causal_conv(uv, st_ref, cw_ref, vlo, nb, tt)
        act_ref[:, glo:glo + CH] = (gate * (1.0 / (1.0 + jnp.exp(-gate))) * val).astype(bf16)
        if glo + CH == split:
            y = y + _dot(act_ref[:, 0:split], wdown_ref[0:split, :])
    y = y + _dot(act_ref[:, split:F], wdown_ref[split:F, :])
    if with_final:
        y = _rms(y, fg_ref[...])
    y_ref[...] = y.reshape(nb, tt, D)
    nctx_ref[...] = st_ref[:, HIST - 2:HIST, :]


def _ffn(x, ctx, g, w_up, cw, w_down, *, nb, tt, attn=None, final_g=None):
    b, t, _ = x.shape
    grid = (b // nb, t // tt)
    tile = pl.BlockSpec((nb, tt, D), lambda i, j: (i, j, 0))
    args, specs = [x], [tile]
    if attn is not None:
        o, w_o = attn
        args += [o, w_o]
        specs += [tile, _const_spec((D, D))]
    args += [ctx, g, w_up, cw, w_down]
    specs += [
        pl.BlockSpec((nb, 2, 2 * F), lambda i, j: (i, 0, 0)),
        _const_spec((1, D)),
        _const_spec((D, 2 * F)),
        _const_spec((3, 2 * F)),
        _const_spec((F, D)),
    ]
    if final_g is not None:
        args.append(final_g)
        specs.append(_const_spec((1, D)))
    return pl.pallas_call(
        functools.partial(_ffn_kernel, nb=nb, tt=tt, with_attn=attn is not None, with_final=final_g is not None),
        grid=grid,
        in_specs=specs,
        out_specs=[tile, pl.BlockSpec((nb, 2, 2 * F), lambda i, j: (i, 0, 0))],
        out_shape=[jax.ShapeDtypeStruct((b, t, D), f32), jax.ShapeDtypeStruct((b, 2, 2 * F), f32)],
        scratch_shapes=[
            pltpu.VMEM((nb * tt, F), bf16),
            pltpu.VMEM((nb, HIST, 2 * F), f32),
        ],
        compiler_params=_params(("parallel", "arbitrary")),
        name="conv_ffn_attn" if attn is not None else "conv_ffn",
    )(*args)


def _split3(c):
    hi = c.astype(bf16).astype(f32)
    r = c - hi
    mid = r.astype(bf16).astype(f32)
    lo = (r - mid).astype(bf16).astype(f32)
    return hi, mid, lo


def _pack3(c):
    hi, mid, lo = _split3(c)
    return hi + pltpu.roll(mid, H, 1) + pltpu.roll(lo, 2 * H, 1)


def _cumsum_cols(logf, carry, tri, eq_ref, ek_ref, oq_ref, ok_ref):
    lane = lax.broadcasted_iota(jnp.int32, logf.shape, 1)
    head = lane < H
    lf = jnp.where(head, logf, 0.0)
    cs = _dot(tri, _pack3(lf).astype(bf16))
    c = cs + pltpu.roll(cs, LANES - H, 1) + pltpu.roll(cs, LANES - 2 * H, 1)
    c = jnp.where(head, c, 0.0) + carry
    cp = _pack3(c * LOG2E).astype(bf16)
    qa = _dot(cp, eq_ref[...]) + oq_ref[...]
    ka = _dot(cp, ek_ref[...]) + ok_ref[...]
    return c, qa.astype(bf16), ka.astype(bf16)


def _tri(nb, tt):
    m = nb * tt
    r = lax.broadcasted_iota(jnp.int32, (m, m), 0)
    c = lax.broadcasted_iota(jnp.int32, (m, m), 1)
    if nb == 1:
        keep = c <= r
    else:
        assert tt & (tt - 1) == 0, "several streams per tile need a power-of-two tile length"
        sh = tt.bit_length() - 1
        keep = (c <= r) & ((r >> sh) == (c >> sh))
    return jnp.where(keep, 1.0, 0.0).astype(bf16)


def _aug_constants():
    eq = np.zeros((LANES, D), np.float32)
    ek = np.zeros((LANES, D), np.float32)
    oq = np.zeros((1, D), np.float32)
    ok = np.zeros((1, D), np.float32)
    for h in range(H):
        base = (h // 2) * LANES + (h % 2) * 2 * PIECES
        for x in range(PIECES):
            eq[x * H + h, base + x] = 1.0
            ok[0, base + x] = 1.0
            oq[0, base + PIECES + x] = 1.0
            ek[x * H + h, base + PIECES + x] = -1.0
    return jnp.asarray(eq, bf16), jnp.asarray(ek, bf16), jnp.asarray(oq), jnp.asarray(ok)


def _proj_kernel(x_ref, c0_ref, kvg_ref, qg_ref, wk_ref, wv_ref, wf_ref, bf_ref, wq_ref,
                 eq_ref, ek_ref, oq_ref, ok_ref,
                 k_ref, v_ref, lf_ref, qm_ref, qa_ref, km_ref, ka_ref, vm_ref, carry_ref, *, nb, tt, v_transposed):
    j = pl.program_id(1)
    m = nb * tt
    x = x_ref[...].reshape(m, D)
    xn = x * lax.rsqrt(jnp.mean(x * x, axis=-1, keepdims=True) + EPS)
    hk = (xn * kvg_ref[...]).astype(bf16)
    hq = (xn * qg_ref[...]).astype(bf16)

    zf = _dot(hk, wf_ref[...])
    k = _dot(hk, wk_ref[...])
    v = _dot(hk, wv_ref[...])
    q = _dot(hq, wq_ref[...])
    k_ref[...] = k.reshape(nb, tt, H, DH)
    v_ref[...] = v.reshape(nb, tt, H, DH)
    km_ref[...] = k.astype(bf16).reshape(nb, tt, D)
    if v_transposed:
        vm_ref[0] = v.T.astype(bf16)
    else:
        vm_ref[...] = v.astype(bf16).reshape(nb, tt, D)
    qm_ref[...] = q.astype(bf16).reshape(nb, tt, D)

    logf = jax.nn.log_sigmoid(zf + bf_ref[...])
    lf_ref[...] = logf[:, 0:H].reshape(nb, tt, H)

    @pl.when(j == 0)
    def _():
        carry_ref[...] = c0_ref[...]

    carry = jnp.broadcast_to(carry_ref[...], (nb, tt, LANES)).reshape(m, LANES)
    c, qa, ka = _cumsum_cols(logf, carry, _tri(nb, tt), eq_ref, ek_ref, oq_ref, ok_ref)
    qa_ref[...] = qa.reshape(nb, tt, D)
    ka_ref[...] = ka.reshape(nb, tt, D)
    carry_ref[...] = c.reshape(nb, tt, LANES)[:, tt - 1:tt, :]


def _proj(x, c0, kvg, qg, wk, wv, wf, bfp, wq, consts, *, nb, tt, v_transposed):
    b, t, _ = x.shape
    grid = (b // nb, t // tt)
    tile = pl.BlockSpec((nb, tt, D), lambda i, j: (i, j, 0))
    heads = pl.BlockSpec((nb, tt, H, DH), lambda i, j: (i, j, 0, 0))
    wide = jax.ShapeDtypeStruct((b, t, H, DH), f32)
    half = jax.ShapeDtypeStruct((b, t, D), bf16)
    if v_transposed:
        assert nb == 1
        vm_spec = pl.BlockSpec((1, D, tt), lambda i, j: (i, 0, j))
        vm_shape = jax.ShapeDtypeStruct((b, D, t), bf16)
    else:
        vm_spec, vm_shape = tile, half
    return pl.pallas_call(
        functools.partial(_proj_kernel, nb=nb, tt=tt, v_transposed=v_transposed),
        grid=grid,
        in_specs=[
            tile,
            pl.BlockSpec((nb, 1, LANES), lambda i, j: (i, 0, 0)),
            _const_spec((1, D)), _const_spec((1, D)),
            _const_spec((D, D)), _const_spec((D, D)), _const_spec((D, LANES)), _const_spec((1, LANES)),
            _const_spec((D, D)),
            _const_spec((LANES, D)), _const_spec((LANES, D)), _const_spec((1, D)), _const_spec((1, D)),
        ],
        out_specs=[heads, heads, pl.BlockSpec((nb, tt, H), lambda i, j: (i, j, 0)), tile, tile, tile, tile, vm_spec],
        out_shape=[wide, wide, jax.ShapeDtypeStruct((b, t, H), f32), half, half, half, half, vm_shape],
        scratch_shapes=[pltpu.VMEM((nb, 1, LANES), f32)],
        compiler_params=_params(("parallel", "arbitrary")),
        name="kvq_proj",
    )(x, c0, kvg, qg, wk, wv, wf, bfp, wq, *consts)


def _cache_aug_kernel(lf_ref, place_ref, eq_ref, ek_ref, oq_ref, ok_ref, ka_ref, ctot_ref, *, p):
    lf16 = lf_ref[0]
    hi, mid, lo = _split3(lf16)
    place = place_ref[...]
    logf = _dot(hi.astype(bf16), place) + _dot(mid.astype(bf16), place) + _dot(lo.astype(bf16), place)
    c, _, ka = _cumsum_cols(logf, jnp.zeros((1, LANES), f32), _tri(1, p), eq_ref, ek_ref, oq_ref, ok_ref)
    ka_ref[0] = ka
    ctot_ref[0] = c[p - 1:p, :]


def _cache_aug(cache_logf, consts):
    b, p, _ = cache_logf.shape
    place = jnp.asarray(np.eye(H, LANES, dtype=np.float32), bf16)
    return pl.pallas_call(
        functools.partial(_cache_aug_kernel, p=p),
        grid=(b,),
        in_specs=[
            pl.BlockSpec((1, p, H), lambda i: (i, 0, 0)),
            _const_spec((H, LANES)),
            _const_spec((LANES, D)), _const_spec((LANES, D)), _const_spec((1, D)), _const_spec((1, D)),
        ],
        out_specs=[pl.BlockSpec((1, p, D), lambda i: (i, 0, 0)), pl.BlockSpec((1, 1, LANES), lambda i: (i, 0, 0))],
        out_shape=[jax.ShapeDtypeStruct((b, p, D), bf16), jax.ShapeDtypeStruct((b, 1, LANES), f32)],
        compiler_params=_params(("parallel",)),
        name="cache_aug",
    )(cache_logf, place, *consts)


def _stack_heads(qm, qa):
    t = qm.shape[0]
    q2 = jnp.concatenate([qm, qa], axis=1).astype(f32)
    lane = lax.broadcasted_iota(jnp.int32, (t, 2 * LANES), 1)
    grp = 2 * PIECES
    keep_a = (lane < DH) | ((lane >= LANES) & (lane < LANES + grp))
    keep_b = ((lane >= DH) & (lane < LANES)) | ((lane >= LANES + grp) & (lane < LANES + 2 * grp))
    return jnp.concatenate([jnp.where(keep_a, q2, 0.0), jnp.where(keep_b, q2, 0.0)], axis=0).astype(bf16)


def _unstack_heads(o, t):
    lane = lax.broadcasted_iota(jnp.int32, (t, LANES), 1)
    return jnp.where(lane < DH, o[0:t], o[t:2 * t])


def _causal(s, t):
    row = lax.broadcasted_iota(jnp.int32, s.shape, 0)
    col = lax.broadcasted_iota(jnp.int32, s.shape, 1)
    row = jnp.where(row >= t, row - t, row)
    return jnp.where(col <= row, s, NEG)


QGRP = 256


AHEAD = 2


def _attn_prompt_kernel(qm_ref, qa_ref, km_ref, ka_ref, vt_ref, o_ref,
                        qs_ref, m_ref, l_ref, acc_ref, s_ref, p_ref, a_ref, *, tq):
    i = pl.program_id(2)
    ngrp = 2 * tq // QGRP
    qs_ref[...] = _stack_heads(qm_ref[0], qa_ref[0])
    m_ref[...] = jnp.full(m_ref.shape, NEG, f32)
    l_ref[...] = jnp.zeros(l_ref.shape, f32)
    acc_ref[...] = jnp.zeros(acc_ref.shape, f32)
    p_ref[...] = jnp.zeros(p_ref.shape, bf16)
    a_ref[...] = jnp.ones(a_ref.shape, f32)

    def keys(jb):
        off = pl.multiple_of(jb * tq, tq)
        return jnp.concatenate([km_ref[0, pl.ds(off, tq), :], ka_ref[0, pl.ds(off, tq), :]], axis=1)

    def values(jb):
        vt = vt_ref[0, :, pl.ds(pl.multiple_of(jb * tq, tq), tq)]
        ones = jnp.ones((ONES_ROWS, tq), bf16)
        return [jnp.concatenate([vt[h * DH:(h + 1) * DH, :], ones], axis=0) for h in range(2)]

    def scores(kblk, g):
        return _dot_nt(kblk, qs_ref[g * QGRP:(g + 1) * QGRP, :])

    def weighted_values(vt_aug, p, alpha, g):
        c0 = g * QGRP
        head, q0 = c0 // tq, c0 % tq
        rows = slice(head * DH, (head + 1) * DH)
        r = _dot(vt_aug[head], p)
        acc_ref[rows, q0:q0 + QGRP] = alpha * acc_ref[rows, q0:q0 + QGRP] + r[0:DH]
        l_ref[:, c0:c0 + QGRP] = alpha * l_ref[:, c0:c0 + QGRP] + r[DH:DH + 1]

    def softmax(s, g, masked):
        c0 = g * QGRP
        if masked:
            key = lax.broadcasted_iota(jnp.int32, s.shape, 0)
            qry = lax.broadcasted_iota(jnp.int32, s.shape, 1) + c0 % tq
            s = jnp.where(key <= qry, s, NEG)
        m_prev = m_ref[:, c0:c0 + QGRP]
        m_new = jnp.maximum(m_prev, jnp.max(s, axis=0, keepdims=True))
        m_ref[:, c0:c0 + QGRP] = m_new
        return jnp.exp2(s - m_new).astype(bf16), jnp.exp2(m_prev - m_new)

    def run(jbs, masked, nxt):
        kb = [keys(jb) for jb in jbs]
        vb = [values(jb) for jb in jbs]
        kn = None if nxt is None else keys(nxt)
        items = [(bi, g) for bi in range(len(jbs)) for g in range(ngrp)]
        ready = {}
        prev = (values(jnp.maximum(jbs[0] - 1, 0)), p_ref[...], a_ref[...], ngrp - 1)
        for t, (bi, g) in enumerate(items):
            s = s_ref[t] if t < AHEAD else ready.pop(t)
            ta = t + AHEAD
            if ta < len(items):
                ready[ta] = scores(kb[items[ta][0]], items[ta][1])
            elif kn is not None:
                s_ref[ta - len(items)] = scores(kn, ta - len(items))
            weighted_values(*prev)
            p, alpha = softmax(s, g, masked)
            prev = (vb[bi], p, alpha, g)
        if nxt is None:
            weighted_values(*prev)
        else:
            p_ref[...] = prev[1]
            a_ref[...] = prev[2]

    def pair_body(t, carry):
        run([2 * t, 2 * t + 1], False, 2 * t + 2)
        return carry

    first = keys(0)
    for g in range(AHEAD):
        s_ref[g] = scores(first, g)
    lax.fori_loop(0, i >> 1, pair_body, 0)

    @pl.when((i & 1) == 1)
    def _():
        run([i - 1], False, i)

    run([i], True, None)
    inv = 1.0 / l_ref[...]
    ot = jnp.concatenate([acc_ref[0:DH, :] * inv[:, 0:tq], acc_ref[DH:2 * DH, :] * inv[:, tq:2 * tq]], axis=0)
    o_ref[0] = ot.T.astype(bf16)


def _attn_prompt(qm, qa, km, ka, vt, *, tq):
    b, t, _ = qm.shape
    assert tq % QGRP == 0 and 2 * tq // QGRP > AHEAD
    qspec = pl.BlockSpec((1, tq, LANES), lambda bi, p, i: (bi, i, p))
    kspec = pl.BlockSpec((1, t, LANES), lambda bi, p, i: (bi, 0, p))
    vspec = pl.BlockSpec((1, LANES, t), lambda bi, p, i: (bi, p, 0))
    return pl.pallas_call(
        functools.partial(_attn_prompt_kernel, tq=tq),
        grid=(b, NPAIR, t // tq),
        in_specs=[qspec, qspec, kspec, kspec, vspec],
        out_specs=qspec,
        out_shape=jax.ShapeDtypeStruct((b, t, D), bf16),
        scratch_shapes=[
            pltpu.VMEM((2 * tq, 2 * LANES), bf16),
            pltpu.VMEM((1, 2 * tq), f32),
            pltpu.VMEM((1, 2 * tq), f32),
            pltpu.VMEM((LANES, tq), f32),
            pltpu.VMEM((AHEAD, tq, QGRP), f32),
            pltpu.VMEM((tq, QGRP), bf16),
            pltpu.VMEM((1, QGRP), f32),
        ],
        compiler_params=_params(("parallel", "parallel", "arbitrary")),
        name="fox_attn_prompt",
    )(qm, qa, km, ka, vt)


def _attn_sample_kernel(qm_ref, qa_ref, kc_ref, kca_ref, vc_ref, kn_ref, kna_ref, vn_ref, o_ref, *, t):
    scores = []
    for pr in range(NPAIR):
        sl = slice(pr * LANES, (pr + 1) * LANES)
        qs = _stack_heads(qm_ref[0, :, sl], qa_ref[0, :, sl])
        kc = jnp.concatenate([kc_ref[0, :, sl], kca_ref[0, :, sl]], axis=1)
        kn = jnp.concatenate([kn_ref[0, :, sl], kna_ref[0, :, sl]], axis=1)
        scores.append((_dot_nt(qs, kc), _dot_nt(qs, kn)))
    for pr, (s_c, s_n) in enumerate(scores):
        sl = slice(pr * LANES, (pr + 1) * LANES)
        s_n = _causal(s_n, t)
        m = jnp.maximum(jnp.max(s_c, axis=1, keepdims=True), jnp.max(s_n, axis=1, keepdims=True))
        p_c = jnp.exp2(s_c - m)
        p_n = jnp.exp2(s_n - m)
        l = jnp.sum(p_c, axis=1, keepdims=True) + jnp.sum(p_n, axis=1, keepdims=True)
        o = _dot(p_c.astype(bf16), vc_ref[0, :, sl]) + _dot(p_n.astype(bf16), vn_ref[0, :, sl])
        o_ref[0, :, sl] = _unstack_heads(o * (1.0 / l), t).astype(bf16)


def _attn_sample(qm, qa, kc, kca, vc, kn, kna, vn):
    b, t, _ = qm.shape
    p = kc.shape[1]
    new = pl.BlockSpec((1, t, D), lambda bi: (bi, 0, 0))
    old = pl.BlockSpec((1, p, D), lambda bi: (bi, 0, 0))
    return pl.pallas_call(
        functools.partial(_attn_sample_kernel, t=t),
        grid=(b,),
        in_specs=[new, new, old, old, old, new, new, new],
        out_specs=new,
        out_shape=jax.ShapeDtypeStruct((b, t, D), bf16),
        compiler_params=_params(("parallel",)),
        name="fox_attn_sample",
    )(qm, qa, kc, kca, vc, kn, kna, vn)


def _trunk(x, sa, sf, past, w, consts, *, nb, tt, tq):
    b, t, _ = x.shape
    x1, nsa = _mixer(x, sa[0], w["a_g"], w["w_in"], w["a_cw"], w["w_out"], nb=nb, tt=tt)
    x2, nsf0 = _ffn(x1, sf[0], w["f_g"][0], w["w_up"][0], w["f_cw"][0], w["w_down"][0], nb=nb, tt=tt)
    if past is None:
        c0 = jnp.zeros((b, 1, LANES), f32)
    else:
        cache_k, cache_v, cache_logf = past
        kca, c0 = _cache_aug(cache_logf, consts)
    k, v, logf, qm, qa, km, ka, vm = _proj(x2, c0, w["kv_g"], w["q_g"], w["w_k"], w["w_v"], w["w_f"], w["b_f"],
                                           w["w_q"], consts, nb=nb, tt=tt, v_transposed=past is None)
    if past is None:
        o = _attn_prompt(qm, qa, km, ka, vm, tq=tq)
    else:
        p = cache_k.shape[1]
        kc = cache_k.reshape(b, p, D).astype(bf16)
        vc = cache_v.reshape(b, p, D).astype(bf16)
        o = _attn_sample(qm, qa, kc, kca, vc, km, ka, vm)
    y, nsf1 = _ffn(x2, sf[1], w["f_g"][1], w["w_up"][1], w["f_cw"][1], w["w_down"][1], nb=nb, tt=tt,
                   attn=(o, w["w_o"]), final_g=w["final_g"])
    return (y, nsa[None], jnp.stack([nsf0, nsf1]), k, v, logf)


def kernel(x_prompt, x_sample, state_conv_a, state_ffn_conv, cache_k, cache_v, cache_logf, a_norm, w_a_in,
           a_conv_w, w_a_out, kv_norm, w_kv, b_f, b_norm, w_q, w_o, ffn_norm, w_ffn_up, ffn_conv_w,
           w_ffn_down, final_norm):
    assert a_norm.shape[0] == 1 and b_norm.shape[0] == 1 and ffn_norm.shape[0] == 2
    w = {
        "a_g": a_norm[0][None], "w_in": w_a_in[0].astype(bf16), "a_cw": a_conv_w[0],
        "w_out": w_a_out[0].astype(bf16),
        "f_g": [ffn_norm[l][None] for l in range(2)],
        "w_up": [w_ffn_up[l].astype(bf16) for l in range(2)],
        "f_cw": [ffn_conv_w[l] for l in range(2)],
        "w_down": [w_ffn_down[l].astype(bf16) for l in range(2)],
        "kv_g": kv_norm[None], "q_g": b_norm[0][None],
        "w_k": w_kv[:, 0:D].astype(bf16), "w_v": w_kv[:, D:2 * D].astype(bf16),
        "w_f": jnp.pad(w_kv[:, 2 * D:], ((0, 0), (0, LANES - H))).astype(bf16),
        "b_f": jnp.pad(b_f, (0, LANES - H))[None],
        "w_q": (w_q[0] * (DH ** -0.5 * LOG2E)).astype(bf16),
        "w_o": w_o[0].astype(bf16), "final_g": final_norm[None],
    }
    consts = _aug_constants()
    bp = x_prompt.shape[0]
    sa0 = jnp.zeros((1, bp, 2, D), f32)
    sf0 = jnp.zeros((2, bp, 2, 2 * F), f32)
    outs_p = _trunk(x_prompt, sa0, sf0, None, w, consts, nb=1, tt=512, tq=512)
    outs_s = _trunk(x_sample, state_conv_a, state_ffn_conv, (cache_k, cache_v, cache_logf), w, consts,
                    nb=16, tt=x_sample.shape[1], tq=None)
    y_p, p_a, p_f, p_k, p_v, p_lf = outs_p
    y_s, s_a, s_f, s_k, s_v, s_lf = outs_s
    return (y_p, y_s, p_a, p_f, p_k, p_v, p_lf, s_a, s_f, s_k, s_v, s_lf)
```

```python
import functools

import numpy as np
import jax
import jax.numpy as jnp
from jax import lax
from jax.experimental import pallas as pl
from jax.experimental.pallas import tpu as pltpu

D = 1024
F = 2816
H = 16
DH = 64
NPAIR = H // 2
LANES = 128
CH = 256
NCH = F // CH
EPS = 1e-6
NEG = -1e30
PIECES = 3
LOG2E = 1.4426950408889634
ONES_ROWS = 16
VMEM_LIMIT = 56 * 1024 * 1024

f32 = jnp.float32
bf16 = jnp.bfloat16


def _const_spec(shape):
    zeros = (0,) * len(shape)
    return pl.BlockSpec(shape, lambda *_: zeros, pipeline_mode=pl.Buffered(1))


def _params(sem):
    return pltpu.CompilerParams(dimension_semantics=sem, vmem_limit_bytes=VMEM_LIMIT)


def _rms(x, g):
    ms = jnp.mean(x * x, axis=-1, keepdims=True)
    return (x * lax.rsqrt(ms + EPS)) * g


def _dot(a, b):
    return jnp.dot(a, b, preferred_element_type=f32)


def _dot_nt(a, b):
    return lax.dot_general(a, b, (((1,), (1,)), ((), ())), preferred_element_type=f32)


HIST = 8


def _init_history(st_ref, ctx_ref):
    st_ref[...] = jnp.zeros(st_ref.shape, f32)
    st_ref[:, HIST - 2:HIST, :] = ctx_ref[...]


def _causal_conv(u, st_ref, cw_ref, lo, nb, tt):
    hi = lo + CH
    u3 = u.reshape(nb, tt, CH)
    rows = nb * (HIST + tt)
    xp = jnp.concatenate([st_ref[:, :, lo:hi], u3], axis=1).reshape(rows, CH)
    cw = cw_ref[:, lo:hi]
    y = pltpu.roll(xp, 2, 0) * cw[0:1] + pltpu.roll(xp, 1, 0) * cw[1:2]
    st_ref[:, HIST - 2:HIST, lo:hi] = u3[:, tt - 2:tt, :]
    return y.reshape(nb, HIST + tt, CH)[:, HIST:, :].reshape(nb * tt, CH) + u * cw[2:3]


def _mix_kernel(x_ref, ctx_ref, g_ref, win_ref, cw_ref, wout_ref, y_ref, nctx_ref, st_ref, zg_ref, *, nb, tt):
    j = pl.program_id(1)
    m = nb * tt
    x = x_ref[...].reshape(m, D)
    h = _rms(x, g_ref[...]).astype(bf16)

    @pl.when(j == 0)
    def _():
        _init_history(st_ref, ctx_ref)

    def up(c):
        lo = c * CH
        return tuple(_dot(h, win_ref[:, part * D + lo:part * D + lo + CH]) for part in range(3))

    nch = D // CH
    y = x
    nxt = up(0)
    for c in range(nch):
        lo = c * CH
        gb, gc, u = nxt
        if c + 1 < nch:
            nxt = up(c + 1)
        z = _causal_conv(gc * u, st_ref, cw_ref, lo, nb, tt)
        zg_ref[:, lo:lo + CH] = (gb * z).astype(bf16)
        if c == nch // 2 - 1:
            y = y + _dot(zg_ref[:, 0:lo + CH], wout_ref[0:lo + CH, :])
    y = y + _dot(zg_ref[:, D // 2:D], wout_ref[D // 2:D, :])
    y_ref[...] = y.reshape(nb, tt, D)
    nctx_ref[...] = st_ref[:, HIST - 2:HIST, :]


def _mixer(x, ctx, g, w_in, cw, w_out, *, nb, tt):
    b, t, _ = x.shape
    grid = (b // nb, t // tt)
    return pl.pallas_call(
        functools.partial(_mix_kernel, nb=nb, tt=tt),
        grid=grid,
        in_specs=[
            pl.BlockSpec((nb, tt, D), lambda i, j: (i, j, 0)),
            pl.BlockSpec((nb, 2, D), lambda i, j: (i, 0, 0)),
            _const_spec((1, D)),
            _const_spec((D, 3 * D)),
            _const_spec((3, D)),
            _const_spec((D, D)),
        ],
        out_specs=[
            pl.BlockSpec((nb, tt, D), lambda i, j: (i, j, 0)),
            pl.BlockSpec((nb, 2, D), lambda i, j: (i, 0, 0)),
        ],
        out_shape=[jax.ShapeDtypeStruct((b, t, D), f32), jax.ShapeDtypeStruct((b, 2, D), f32)],
        scratch_shapes=[pltpu.VMEM((nb, HIST, D), f32), pltpu.VMEM((nb * tt, D), bf16)],
        compiler_params=_params(("parallel", "arbitrary")),
        name="mixer_a",
    )(x, ctx, g, w_in, cw, w_out)


def _ffn_kernel(*refs, nb, tt, with_attn, with_final):
    refs = list(refs)
    x_ref = refs.pop(0)
    if with_attn:
        o_ref, wo_ref = refs.pop(0), refs.pop(0)
    ctx_ref, g_ref, wup_ref, cw_ref, wdown_ref = (refs.pop(0) for _ in range(5))
    if with_final:
        fg_ref = refs.pop(0)
    y_ref, nctx_ref, act_ref, st_ref = refs

    j = pl.program_id(1)
    m = nb * tt
    x = x_ref[...].reshape(m, D)
    if with_attn:
        x = x + _dot(o_ref[...].reshape(m, D), wo_ref[...])
    h = _rms(x, g_ref[...]).astype(bf16)

    @pl.when(j == 0)
    def _():
        _init_history(st_ref, ctx_ref)

    def up(c):
        glo, vlo = c * CH, F + c * CH
        return _dot(h, wup_ref[:, glo:glo + CH]), _dot(h, wup_ref[:, vlo:vlo + CH])

    split = (NCH + 1) // 2 * CH
    y = x
    nxt = up(0)
    for c in range(NCH):
        glo, vlo = c * CH, F + c * CH
        ug, uv = nxt
        if c + 1 < NCH:
            nxt = up(c + 1)
        gate = _causal_conv(ug, st_ref, cw_ref, glo, nb, tt)
        val = _causal_conv(uv, st_ref, cw_ref, vlo, nb, tt)
        act_ref[:, glo:glo + CH] = (gate * (1.0 / (1.0 + jnp.exp(-gate))) * val).astype(bf16)
        if glo + CH == split:
            y = y + _dot(act_ref[:, 0:split], wdown_ref[0:split, :])
    y = y + _dot(act_ref[:, split:F], wdown_ref[split:F, :])
    if with_final:
        y = _rms(y, fg_ref[...])
    y_ref[...] = y.reshape(nb, tt, D)
    nctx_ref[...] = st_ref[:, HIST - 2:HIST, :]


def _ffn(x, ctx, g, w_up, cw, w_down, *, nb, tt, attn=None, final_g=None):
    b, t, _ = x.shape
    grid = (b // nb, t // tt)
    tile = pl.BlockSpec((nb, tt, D), lambda i, j: (i, j, 0))
    args, specs = [x], [tile]
    if attn is not None:
        o, w_o = attn
        args += [o, w_o]
        specs += [tile, _const_spec((D, D))]
    args += [ctx, g, w_up, cw, w_down]
    specs += [
        pl.BlockSpec((nb, 2, 2 * F), lambda i, j: (i, 0, 0)),
        _const_spec((1, D)),
        _const_spec((D, 2 * F)),
        _const_spec((3, 2 * F)),
        _const_spec((F, D)),
    ]
    if final_g is not None:
        args.append(final_g)
        specs.append(_const_spec((1, D)))
    return pl.pallas_call(
        functools.partial(_ffn_kernel, nb=nb, tt=tt, with_attn=attn is not None, with_final=final_g is not None),
        grid=grid,
        in_specs=specs,
        out_specs=[tile, pl.BlockSpec((nb, 2, 2 * F), lambda i, j: (i, 0, 0))],
        out_shape=[jax.ShapeDtypeStruct((b, t, D), f32), jax.ShapeDtypeStruct((b, 2, 2 * F), f32)],
        scratch_shapes=[
            pltpu.VMEM((nb * tt, F), bf16),
            pltpu.VMEM((nb, HIST, 2 * F), f32),
        ],
        compiler_params=_params(("parallel", "arbitrary")),
        name="conv_ffn_attn" if attn is not None else "conv_ffn",
    )(*args)


def _split3(c):
    hi = c.astype(bf16).astype(f32)
    r = c - hi
    mid = r.astype(bf16).astype(f32)
    lo = (r - mid).astype(bf16).astype(f32)
    return hi, mid, lo


def _pack3(c):
    hi, mid, lo = _split3(c)
    return hi + pltpu.roll(mid, H, 1) + pltpu.roll(lo, 2 * H, 1)


CSUM_ROWS = 256


def _running_sum(pk, nb, tt):
    m = nb * tt
    if m <= CSUM_ROWS:
        return _dot(_tri(nb, tt), pk)
    assert m % CSUM_ROWS == 0
    if nb == 1:
        tri = _tri(1, CSUM_ROWS)
        blocks, run = [], jnp.zeros((1, LANES), f32)
        for r0 in range(0, m, CSUM_ROWS):
            blk = _dot(tri, pk[r0:r0 + CSUM_ROWS]) + run
            blocks.append(blk)
            run = blk[CSUM_ROWS - 1:CSUM_ROWS]
        return jnp.concatenate(blocks, axis=0)
    assert CSUM_ROWS % tt == 0
    tri = _tri(CSUM_ROWS // tt, tt)
    return jnp.concatenate([_dot(tri, pk[r0:r0 + CSUM_ROWS]) for r0 in range(0, m, CSUM_ROWS)], axis=0)


def _cumsum_cols(logf, carry, nb, tt, eq_ref, ek_ref, oq_ref, ok_ref):
    lane = lax.broadcasted_iota(jnp.int32, logf.shape, 1)
    head = lane < H
    lf = jnp.where(head, logf, 0.0)
    cs = _running_sum(_pack3(lf).astype(bf16), nb, tt)
    c = cs + pltpu.roll(cs, LANES - H, 1) + pltpu.roll(cs, LANES - 2 * H, 1)
    c = jnp.where(head, c, 0.0) + carry
    cp = _pack3(c * LOG2E).astype(bf16)
    qa = _dot(cp, eq_ref[...]) + oq_ref[...]
    ka = _dot(cp, ek_ref[...]) + ok_ref[...]
    return c, qa.astype(bf16), ka.astype(bf16)


def _tri(nb, tt):
    m = nb * tt
    r = lax.broadcasted_iota(jnp.int32, (m, m), 0)
    c = lax.broadcasted_iota(jnp.int32, (m, m), 1)
    if nb == 1:
        keep = c <= r
    else:
        assert tt & (tt - 1) == 0, "several streams per tile need a power-of-two tile length"
        sh = tt.bit_length() - 1
        keep = (c <= r) & ((r >> sh) == (c >> sh))
    return jnp.where(keep, 1.0, 0.0).astype(bf16)


def _aug_constants():
    eq = np.zeros((LANES, D), np.float32)
    ek = np.zeros((LANES, D), np.float32)
    oq = np.zeros((1, D), np.float32)
    ok = np.zeros((1, D), np.float32)
    for h in range(H):
        base = (h // 2) * LANES + (h % 2) * 2 * PIECES
        for x in range(PIECES):
            eq[x * H + h, base + x] = 1.0
            ok[0, base + x] = 1.0
            oq[0, base + PIECES + x] = 1.0
            ek[x * H + h, base + PIECES + x] = -1.0
    return jnp.asarray(eq, bf16), jnp.asarray(ek, bf16), jnp.asarray(oq), jnp.asarray(ok)


def _proj_kernel(x_ref, c0_ref, kvg_ref, qg_ref, wk_ref, wv_ref, wf_ref, bf_ref, wq_ref,
                 eq_ref, ek_ref, oq_ref, ok_ref,
                 k_ref, v_ref, lf_ref, qm_ref, qa_ref, km_ref, ka_ref, vm_ref, carry_ref, *, nb, tt, v_transposed):
    j = pl.program_id(1)
    m = nb * tt
    x = x_ref[...].reshape(m, D)
    xn = x * lax.rsqrt(jnp.mean(x * x, axis=-1, keepdims=True) + EPS)
    hk = (xn * kvg_ref[...]).astype(bf16)
    hq = (xn * qg_ref[...]).astype(bf16)

    zf = _dot(hk, wf_ref[...])
    k = _dot(hk, wk_ref[...])
    v = _dot(hk, wv_ref[...])
    q = _dot(hq, wq_ref[...])
    k_ref[...] = k.reshape(nb, tt, H, DH)
    v_ref[...] = v.reshape(nb, tt, H, DH)
    km_ref[...] = k.astype(bf16).reshape(nb, tt, D)
    if v_transposed:
        vm_ref[0] = v.T.astype(bf16)
    else:
        vm_ref[...] = v.astype(bf16).reshape(nb, tt, D)
    qm_ref[...] = q.astype(bf16).reshape(nb, tt, D)

    logf = jax.nn.log_sigmoid(zf + bf_ref[...])
    lf_ref[...] = logf[:, 0:H].reshape(nb, tt, H)

    @pl.when(j == 0)
    def _():
        carry_ref[...] = c0_ref[...]

    carry = jnp.broadcast_to(carry_ref[...], (nb, tt, LANES)).reshape(m, LANES)
    c, qa, ka = _cumsum_cols(logf, carry, nb, tt, eq_ref, ek_ref, oq_ref, ok_ref)
    qa_ref[...] = qa.reshape(nb, tt, D)
    ka_ref[...] = ka.reshape(nb, tt, D)
    carry_ref[...] = c.reshape(nb, tt, LANES)[:, tt - 1:tt, :]


def _proj(x, c0, kvg, qg, wk, wv, wf, bfp, wq, consts, *, nb, tt, v_transposed):
    b, t, _ = x.shape
    grid = (b // nb, t // tt)
    tile = pl.BlockSpec((nb, tt, D), lambda i, j: (i, j, 0))
    heads = pl.BlockSpec((nb, tt, H, DH), lambda i, j: (i, j, 0, 0))
    wide = jax.ShapeDtypeStruct((b, t, H, DH), f32)
    half = jax.ShapeDtypeStruct((b, t, D), bf16)
    if v_transposed:
        assert nb == 1
        vm_spec = pl.BlockSpec((1, D, tt), lambda i, j: (i, 0, j))
        vm_shape = jax.ShapeDtypeStruct((b, D, t), bf16)
    else:
        vm_spec, vm_shape = tile, half
    return pl.pallas_call(
        functools.partial(_proj_kernel, nb=nb, tt=tt, v_transposed=v_transposed),
        grid=grid,
        in_specs=[
            tile,
            pl.BlockSpec((nb, 1, LANES), lambda i, j: (i, 0, 0)),
            _const_spec((1, D)), _const_spec((1, D)),
            _const_spec((D, D)), _const_spec((D, D)), _const_spec((D, LANES)), _const_spec((1, LANES)),
            _const_spec((D, D)),
            _const_spec((LANES, D)), _const_spec((LANES, D)), _const_spec((1, D)), _const_spec((1, D)),
        ],
        out_specs=[heads, heads, pl.BlockSpec((nb, tt, H), lambda i, j: (i, j, 0)), tile, tile, tile, tile, vm_spec],
        out_shape=[wide, wide, jax.ShapeDtypeStruct((b, t, H), f32), half, half, half, half, vm_shape],
        scratch_shapes=[pltpu.VMEM((nb, 1, LANES), f32)],
        compiler_params=_params(("parallel", "arbitrary")),
        name="kvq_proj",
    )(x, c0, kvg, qg, wk, wv, wf, bfp, wq, *consts)


def _cache_aug_kernel(lf_ref, place_ref, eq_ref, ek_ref, oq_ref, ok_ref, ka_ref, ctot_ref, *, p):
    lf16 = lf_ref[0]
    hi, mid, lo = _split3(lf16)
    place = place_ref[...]
    logf = _dot(hi.astype(bf16), place) + _dot(mid.astype(bf16), place) + _dot(lo.astype(bf16), place)
    c, _, ka = _cumsum_cols(logf, jnp.zeros((1, LANES), f32), 1, p, eq_ref, ek_ref, oq_ref, ok_ref)
    ka_ref[0] = ka
    ctot_ref[0] = c[p - 1:p, :]


def _cache_aug(cache_logf, consts):
    b, p, _ = cache_logf.shape
    place = jnp.asarray(np.eye(H, LANES, dtype=np.float32), bf16)
    return pl.pallas_call(
        functools.partial(_cache_aug_kernel, p=p),
        grid=(b,),
        in_specs=[
            pl.BlockSpec((1, p, H), lambda i: (i, 0, 0)),
            _const_spec((H, LANES)),
            _const_spec((LANES, D)), _const_spec((LANES, D)), _const_spec((1, D)), _const_spec((1, D)),
        ],
        out_specs=[pl.BlockSpec((1, p, D), lambda i: (i, 0, 0)), pl.BlockSpec((1, 1, LANES), lambda i: (i, 0, 0))],
        out_shape=[jax.ShapeDtypeStruct((b, p, D), bf16), jax.ShapeDtypeStruct((b, 1, LANES), f32)],
        compiler_params=_params(("parallel",)),
        name="cache_aug",
    )(cache_logf, place, *consts)


def _stack_heads(qm, qa):
    t = qm.shape[0]
    q2 = jnp.concatenate([qm, qa], axis=1).astype(f32)
    lane = lax.broadcasted_iota(jnp.int32, (t, 2 * LANES), 1)
    grp = 2 * PIECES
    keep_a = (lane < DH) | ((lane >= LANES) & (lane < LANES + grp))
    keep_b = ((lane >= DH) & (lane < LANES)) | ((lane >= LANES + grp) & (lane < LANES + 2 * grp))
    return jnp.concatenate([jnp.where(keep_a, q2, 0.0), jnp.where(keep_b, q2, 0.0)], axis=0).astype(bf16)


def _unstack_heads(o, t):
    lane = lax.broadcasted_iota(jnp.int32, (t, LANES), 1)
    return jnp.where(lane < DH, o[0:t], o[t:2 * t])


def _causal(s, t):
    row = lax.broadcasted_iota(jnp.int32, s.shape, 0)
    col = lax.broadcasted_iota(jnp.int32, s.shape, 1)
    row = jnp.where(row >= t, row - t, row)
    return jnp.where(col <= row, s, NEG)


QGRP = 256


AHEAD = 2


def _attn_prompt_kernel(qm_ref, qa_ref, km_ref, ka_ref, vt_ref, o_ref,
                        qs_ref, m_ref, l_ref, acc_ref, s_ref, p_ref, a_ref, *, tq):
    i = pl.program_id(2)
    ngrp = 2 * tq // QGRP
    qs_ref[...] = _stack_heads(qm_ref[0], qa_ref[0])
    m_ref[...] = jnp.full(m_ref.shape, NEG, f32)
    l_ref[...] = jnp.zeros(l_ref.shape, f32)
    acc_ref[...] = jnp.zeros(acc_ref.shape, f32)
    p_ref[...] = jnp.zeros(p_ref.shape, bf16)
    a_ref[...] = jnp.ones(a_ref.shape, f32)

    def keys(jb):
        off = pl.multiple_of(jb * tq, tq)
        return jnp.concatenate([km_ref[0, pl.ds(off, tq), :], ka_ref[0, pl.ds(off, tq), :]], axis=1)

    def values(jb):
        vt = vt_ref[0, :, pl.ds(pl.multiple_of(jb * tq, tq), tq)]
        ones = jnp.ones((ONES_ROWS, tq), bf16)
        return [jnp.concatenate([vt[h * DH:(h + 1) * DH, :], ones], axis=0) for h in range(2)]

    def scores(kblk, g):
        return _dot_nt(kblk, qs_ref[g * QGRP:(g + 1) * QGRP, :])

    def weighted_values(vt_aug, p, alpha, g):
        c0 = g * QGRP
        head, q0 = c0 // tq, c0 % tq
        rows = slice(head * DH, (head + 1) * DH)
        r = _dot(vt_aug[head], p)
        acc_ref[rows, q0:q0 + QGRP] = alpha * acc_ref[rows, q0:q0 + QGRP] + r[0:DH]
        l_ref[:, c0:c0 + QGRP] = alpha * l_ref[:, c0:c0 + QGRP] + r[DH:DH + 1]

    def softmax(s, g, masked):
        c0 = g * QGRP
        if masked:
            key = lax.broadcasted_iota(jnp.int32, s.shape, 0)
            qry = lax.broadcasted_iota(jnp.int32, s.shape, 1) + c0 % tq
            s = jnp.where(key <= qry, s, NEG)
        m_prev = m_ref[:, c0:c0 + QGRP]
        m_new = jnp.maximum(m_prev, jnp.max(s, axis=0, keepdims=True))
        m_ref[:, c0:c0 + QGRP] = m_new
        return jnp.exp2(s - m_new).astype(bf16), jnp.exp2(m_prev - m_new)

    def run(jbs, masked, nxt):
        kb = [keys(jb) for jb in jbs]
        vb = [values(jb) for jb in jbs]
        kn = None if nxt is None else keys(nxt)
        items = [(bi, g) for bi in range(len(jbs)) for g in range(ngrp)]
        ready = {}
        prev = (values(jnp.maximum(jbs[0] - 1, 0)), p_ref[...], a_ref[...], ngrp - 1)
        for t, (bi, g) in enumerate(items):
            s = s_ref[t] if t < AHEAD else ready.pop(t)
            ta = t + AHEAD
            if ta < len(items):
                ready[ta] = scores(kb[items[ta][0]], items[ta][1])
            elif kn is not None:
                s_ref[ta - len(items)] = scores(kn, ta - len(items))
            weighted_values(*prev)
            p, alpha = softmax(s, g, masked)
            prev = (vb[bi], p, alpha, g)
        if nxt is None:
            weighted_values(*prev)
        else:
            p_ref[...] = prev[1]
            a_ref[...] = prev[2]

    def pair_body(t, carry):
        run([2 * t, 2 * t + 1], False, 2 * t + 2)
        return carry

    first = keys(0)
    for g in range(AHEAD):
        s_ref[g] = scores(first, g)
    lax.fori_loop(0, i >> 1, pair_body, 0)

    @pl.when((i & 1) == 1)
    def _():
        run([i - 1], False, i)

    run([i], True, None)
    inv = 1.0 / l_ref[...]
    ot = jnp.concatenate([acc_ref[0:DH, :] * inv[:, 0:tq], acc_ref[DH:2 * DH, :] * inv[:, tq:2 * tq]], axis=0)
    o_ref[0] = ot.T.astype(bf16)


def _attn_prompt(qm, qa, km, ka, vt, *, tq):
    b, t, _ = qm.shape
    assert tq % QGRP == 0 and 2 * tq // QGRP > AHEAD
    qspec = pl.BlockSpec((1, tq, LANES), lambda bi, p, i: (bi, i, p))
    kspec = pl.BlockSpec((1, t, LANES), lambda bi, p, i: (bi, 0, p))
    vspec = pl.BlockSpec((1, LANES, t), lambda bi, p, i: (bi, p, 0))
    return pl.pallas_call(
        functools.partial(_attn_prompt_kernel, tq=tq),
        grid=(b, NPAIR, t // tq),
        in_specs=[qspec, qspec, kspec, kspec, vspec],
        out_specs=qspec,
        out_shape=jax.ShapeDtypeStruct((b, t, D), bf16),
        scratch_shapes=[
            pltpu.VMEM((2 * tq, 2 * LANES), bf16),
            pltpu.VMEM((1, 2 * tq), f32),
            pltpu.VMEM((1, 2 * tq), f32),
            pltpu.VMEM((LANES, tq), f32),
            pltpu.VMEM((AHEAD, tq, QGRP), f32),
            pltpu.VMEM((tq, QGRP), bf16),
            pltpu.VMEM((1, QGRP), f32),
        ],
        compiler_params=_params(("parallel", "parallel", "arbitrary")),
        name="fox_attn_prompt",
    )(qm, qa, km, ka, vt)


def _attn_sample_kernel(qm_ref, qa_ref, kc_ref, kca_ref, vc_ref, kn_ref, kna_ref, vn_ref, o_ref, *, t):
    p = kc_ref.shape[1]
    kc_all = kc_ref[0].astype(bf16).reshape(p, D)
    vc_all = vc_ref[0].astype(bf16).reshape(p, D)
    scores = []
    for pr in range(NPAIR):
        sl = slice(pr * LANES, (pr + 1) * LANES)
        qs = _stack_heads(qm_ref[0, :, sl], qa_ref[0, :, sl])
        kc = jnp.concatenate([kc_all[:, sl], kca_ref[0, :, sl]], axis=1)
        kn = jnp.concatenate([kn_ref[0, :, sl], kna_ref[0, :, sl]], axis=1)
        scores.append((_dot_nt(qs, kc), _dot_nt(qs, kn)))
    for pr, (s_c, s_n) in enumerate(scores):
        sl = slice(pr * LANES, (pr + 1) * LANES)
        s_n = _causal(s_n, t)
        m = jnp.maximum(jnp.max(s_c, axis=1, keepdims=True), jnp.max(s_n, axis=1, keepdims=True))
        p_c = jnp.exp2(s_c - m)
        p_n = jnp.exp2(s_n - m)
        l = jnp.sum(p_c, axis=1, keepdims=True) + jnp.sum(p_n, axis=1, keepdims=True)
        o = _dot(p_c.astype(bf16), vc_all[:, sl]) + _dot(p_n.astype(bf16), vn_ref[0, :, sl])
        o_ref[0, :, sl] = _unstack_heads(o * (1.0 / l), t).astype(bf16)


def _attn_sample(qm, qa, kc, kca, vc, kn, kna, vn):
    b, t, _ = qm.shape
    p = kc.shape[1]
    new = pl.BlockSpec((1, t, D), lambda bi: (bi, 0, 0))
    old = pl.BlockSpec((1, p, D), lambda bi: (bi, 0, 0))
    raw = pl.BlockSpec((1, p, H, DH), lambda bi: (bi, 0, 0, 0))
    return pl.pallas_call(
        functools.partial(_attn_sample_kernel, t=t),
        grid=(b,),
        in_specs=[new, new, raw, old, raw, new, new, new],
        out_specs=new,
        out_shape=jax.ShapeDtypeStruct((b, t, D), bf16),
        compiler_params=_params(("parallel",)),
        name="fox_attn_sample",
    )(qm, qa, kc, kca, vc, kn, kna, vn)


def _trunk(x, sa, sf, past, w, consts, *, nb, tt, tq):
    b, t, _ = x.shape
    x1, nsa = _mixer(x, sa[0], w["a_g"], w["w_in"], w["a_cw"], w["w_out"], nb=nb, tt=tt)
    x2, nsf0 = _ffn(x1, sf[0], w["f_g"][0], w["w_up"][0], w["f_cw"][0], w["w_down"][0], nb=nb, tt=tt)
    if past is None:
        c0 = jnp.zeros((b, 1, LANES), f32)
    else:
        cache_k, cache_v, cache_logf = past
        kca, c0 = _cache_aug(cache_logf, consts)
    k, v, logf, qm, qa, km, ka, vm = _proj(x2, c0, w["kv_g"], w["q_g"], w["w_k"], w["w_v"], w["w_f"], w["b_f"],
                                           w["w_q"], consts, nb=nb, tt=tt, v_transposed=past is None)
    if past is None:
        o = _attn_prompt(qm, qa, km, ka, vm, tq=tq)
    else:
        p = cache_k.shape[1]
        o = _attn_sample(qm, qa, cache_k, kca, cache_v, km, ka, vm)
    y, nsf1 = _ffn(x2, sf[1], w["f_g"][1], w["w_up"][1], w["f_cw"][1], w["w_down"][1], nb=nb, tt=tt,
                   attn=(o, w["w_o"]), final_g=w["final_g"])
    return (y, nsa[None], jnp.stack([nsf0, nsf1]), k, v, logf)


def kernel(x_prompt, x_sample, state_conv_a, state_ffn_conv, cache_k, cache_v, cache_logf, a_norm, w_a_in,
           a_conv_w, w_a_out, kv_norm, w_kv, b_f, b_norm, w_q, w_o, ffn_norm, w_ffn_up, ffn_conv_w,
           w_ffn_down, final_norm):
    assert a_norm.shape[0] == 1 and b_norm.shape[0] == 1 and ffn_norm.shape[0] == 2
    w = {
        "a_g": a_norm[0][None], "w_in": w_a_in[0].astype(bf16), "a_cw": a_conv_w[0],
        "w_out": w_a_out[0].astype(bf16),
        "f_g": [ffn_norm[l][None] for l in range(2)],
        "w_up": [w_ffn_up[l].astype(bf16) for l in range(2)],
        "f_cw": [ffn_conv_w[l] for l in range(2)],
        "w_down": [w_ffn_down[l].astype(bf16) for l in range(2)],
        "kv_g": kv_norm[None], "q_g": b_norm[0][None],
        "w_k": w_kv[:, 0:D].astype(bf16), "w_v": w_kv[:, D:2 * D].astype(bf16),
        "w_f": jnp.pad(w_kv[:, 2 * D:], ((0, 0), (0, LANES - H))).astype(bf16),
        "b_f": jnp.pad(b_f, (0, LANES - H))[None],
        "w_q": (w_q[0] * (DH ** -0.5 * LOG2E)).astype(bf16),
        "w_o": w_o[0].astype(bf16), "final_g": final_norm[None],
    }
    consts = _aug_constants()
    bp = x_prompt.shape[0]
    sa0 = jnp.zeros((1, bp, 2, D), f32)
    sf0 = jnp.zeros((2, bp, 2, 2 * F), f32)
    outs_p = _trunk(x_prompt, sa0, sf0, None, w, consts, nb=1, tt=512, tq=512)
    outs_s = _trunk(x_sample, state_conv_a, state_ffn_conv, (cache_k, cache_v, cache_logf), w, consts,
                    nb=16, tt=x_sample.shape[1], tq=None)
    y_p, p_a, p_f, p_k, p_v, p_lf = outs_p
    y_s, s_a, s_f, s_k, s_v, s_lf = outs_s
    return (y_p, y_s, p_a, p_f, p_k, p_v, p_lf, s_a, s_f, s_k, s_v, s_lf)
```

```python
import functools

import numpy as np
import jax
import jax.numpy as jnp
from jax import lax
from jax.experimental import pallas as pl
from jax.experimental.pallas import tpu as pltpu

D = 1024
F = 2816
H = 16
DH = 64
NPAIR = H // 2
LANES = 128
CH = 256
NCH = F // CH
EPS = 1e-6
NEG = -1e30
PIECES = 3
LOG2E = 1.4426950408889634
ONES_ROWS = 16
VMEM_LIMIT = 56 * 1024 * 1024

f32 = jnp.float32
bf16 = jnp.bfloat16


def _const_spec(shape):
    zeros = (0,) * len(shape)
    return pl.BlockSpec(shape, lambda *_: zeros, pipeline_mode=pl.Buffered(1))


def _params(sem):
    return pltpu.CompilerParams(dimension_semantics=sem, vmem_limit_bytes=VMEM_LIMIT)


def _rms(x, g):
    ms = jnp.mean(x * x, axis=-1, keepdims=True)
    return (x * lax.rsqrt(ms + EPS)) * g


def _dot(a, b):
    return jnp.dot(a, b, preferred_element_type=f32)


def _dot_nt(a, b):
    return lax.dot_general(a, b, (((1,), (1,)), ((), ())), preferred_element_type=f32)


HIST = 8


def _init_history(st_ref, ctx_ref):
    st_ref[...] = jnp.zeros(st_ref.shape, f32)
    st_ref[:, HIST - 2:HIST, :] = ctx_ref[...]


def _causal_conv(u, st_ref, cw_ref, lo, nb, tt):
    hi = lo + CH
    u3 = u.reshape(nb, tt, CH)
    rows = nb * (HIST + tt)
    xp = jnp.concatenate([st_ref[:, :, lo:hi], u3], axis=1).reshape(rows, CH)
    cw = cw_ref[:, lo:hi]
    y = pltpu.roll(xp, 2, 0) * cw[0:1] + pltpu.roll(xp, 1, 0) * cw[1:2]
    st_ref[:, HIST - 2:HIST, lo:hi] = u3[:, tt - 2:tt, :]
    return y.reshape(nb, HIST + tt, CH)[:, HIST:, :].reshape(nb * tt, CH) + u * cw[2:3]


def _mix_kernel(x_ref, ctx_ref, g_ref, win_ref, cw_ref, wout_ref, y_ref, nctx_ref, st_ref, zg_ref, *, nb, tt):
    j = pl.program_id(1)
    m = nb * tt
    x = x_ref[...].reshape(m, D)
    h = _rms(x, g_ref[...]).astype(bf16)

    @pl.when(j == 0)
    def _():
        _init_history(st_ref, ctx_ref)

    def up(c):
        lo = c * CH
        return tuple(_dot(h, win_ref[:, part * D + lo:part * D + lo + CH]) for part in range(3))

    nch = D // CH
    y = x
    nxt = up(0)
    for c in range(nch):
        lo = c * CH
        gb, gc, u = nxt
        if c + 1 < nch:
            nxt = up(c + 1)
        z = _causal_conv(gc * u, st_ref, cw_ref, lo, nb, tt)
        zg_ref[:, lo:lo + CH] = (gb * z).astype(bf16)
        if c == nch // 2 - 1:
            y = y + _dot(zg_ref[:, 0:lo + CH], wout_ref[0:lo + CH, :])
    y = y + _dot(zg_ref[:, D // 2:D], wout_ref[D // 2:D, :])
    y_ref[...] = y.reshape(nb, tt, D)
    nctx_ref[...] = st_ref[:, HIST - 2:HIST, :]


def _mixer(x, ctx, g, w_in, cw, w_out, *, nb, tt):
    b, t, _ = x.shape
    grid = (b // nb, t // tt)
    return pl.pallas_call(
        functools.partial(_mix_kernel, nb=nb, tt=tt),
        grid=grid,
        in_specs=[
            pl.BlockSpec((nb, tt, D), lambda i, j: (i, j, 0)),
            pl.BlockSpec((nb, 2, D), lambda i, j: (i, 0, 0)),
            _const_spec((1, D)),
            _const_spec((D, 3 * D)),
            _const_spec((3, D)),
            _const_spec((D, D)),
        ],
        out_specs=[
            pl.BlockSpec((nb, tt, D), lambda i, j: (i, j, 0)),
            pl.BlockSpec((nb, 2, D), lambda i, j: (i, 0, 0)),
        ],
        out_shape=[jax.ShapeDtypeStruct((b, t, D), f32), jax.ShapeDtypeStruct((b, 2, D), f32)],
        scratch_shapes=[pltpu.VMEM((nb, HIST, D), f32), pltpu.VMEM((nb * tt, D), bf16)],
        compiler_params=_params(("parallel", "arbitrary")),
        name="mixer_a",
    )(x, ctx, g, w_in, cw, w_out)


def _ffn_kernel(*refs, nb, tt, with_attn, with_final):
    refs = list(refs)
    x_ref = refs.pop(0)
    if with_attn:
        o_ref, wo_ref = refs.pop(0), refs.pop(0)
    ctx_ref, g_ref, wup_ref, cw_ref, wdown_ref = (refs.pop(0) for _ in range(5))
    if with_final:
        fg_ref = refs.pop(0)
    y_ref, nctx_ref, act_ref, st_ref = refs

    j = pl.program_id(1)
    m = nb * tt
    x = x_ref[...].reshape(m, D)
    if with_attn:
        x = x + _dot(o_ref[...].reshape(m, D), wo_ref[...])
    h = _rms(x, g_ref[...]).astype(bf16)

    @pl.when(j == 0)
    def _():
        _init_history(st_ref, ctx_ref)

    def up(c):
        glo, vlo = c * CH, F + c * CH
        return _dot(h, wup_ref[:, glo:glo + CH]), _dot(h, wup_ref[:, vlo:vlo + CH])

    split = (NCH + 1) // 2 * CH
    y = x
    nxt = up(0)
    for c in range(NCH):
        glo, vlo = c * CH, F + c * CH
        ug, uv = nxt
        if c + 1 < NCH:
            nxt = up(c + 1)
        gate = _causal_conv(ug, st_ref, cw_ref, glo, nb, tt)
        val = _causal_conv(uv, st_ref, cw_ref, vlo, nb, tt)
        act_ref[:, glo:glo + CH] = (gate * (1.0 / (1.0 + jnp.exp(-gate))) * val).astype(bf16)
        if glo + CH == split:
            y = y + _dot(act_ref[:, 0:split], wdown_ref[0:split, :])
    y = y + _dot(act_ref[:, split:F], wdown_ref[split:F, :])
    if with_final:
        y = _rms(y, fg_ref[...])
    y_ref[...] = y.reshape(nb, tt, D)
    nctx_ref[...] = st_ref[:, HIST - 2:HIST, :]


def _ffn(x, ctx, g, w_up, cw, w_down, *, nb, tt, attn=None, final_g=None):
    b, t, _ = x.shape
    grid = (b // nb, t // tt)
    tile = pl.BlockSpec((nb, tt, D), lambda i, j: (i, j, 0))
    args, specs = [x], [tile]
    if attn is not None:
        o, w_o = attn
        args += [o, w_o]
        specs += [tile, _const_spec((D, D))]
    args += [ctx, g, w_up, cw, w_down]
    specs += [
        pl.BlockSpec((nb, 2, 2 * F), lambda i, j: (i, 0, 0)),
        _const_spec((1, D)),
        _const_spec((D, 2 * F)),
        _const_spec((3, 2 * F)),
        _const_spec((F, D)),
    ]
    if final_g is not None:
        args.append(final_g)
        specs.append(_const_spec((1, D)))
    return pl.pallas_call(
        functools.partial(_ffn_kernel, nb=nb, tt=tt, with_attn=attn is not None, with_final=final_g is not None),
        grid=grid,
        in_specs=specs,
        out_specs=[tile, pl.BlockSpec((nb, 2, 2 * F), lambda i, j: (i, 0, 0))],
        out_shape=[jax.ShapeDtypeStruct((b, t, D), f32), jax.ShapeDtypeStruct((b, 2, 2 * F), f32)],
        scratch_shapes=[
            pltpu.VMEM((nb * tt, F), bf16),
            pltpu.VMEM((nb, HIST, 2 * F), f32),
        ],
        compiler_params=_params(("parallel", "arbitrary")),
        name="conv_ffn_attn" if attn is not None else "conv_ffn",
    )(*args)


def _split3(c):
    hi = c.astype(bf16).astype(f32)
    r = c - hi
    mid = r.astype(bf16).astype(f32)
    lo = (r - mid).astype(bf16).astype(f32)
    return hi, mid, lo


def _pack3(c):
    hi, mid, lo = _split3(c)
    return hi + pltpu.roll(mid, H, 1) + pltpu.roll(lo, 2 * H, 1)


CSUM_ROWS = 256


def _running_sum(pk, nb, tt):
    m = nb * tt
    if m <= CSUM_ROWS:
        return _dot(_tri(nb, tt), pk)
    assert m % CSUM_ROWS == 0
    if nb == 1:
        tri = _tri(1, CSUM_ROWS)
        blocks, run = [], jnp.zeros((1, LANES), f32)
        for r0 in range(0, m, CSUM_ROWS):
            blk = _dot(tri, pk[r0:r0 + CSUM_ROWS]) + run
            blocks.append(blk)
            run = blk[CSUM_ROWS - 1:CSUM_ROWS]
        return jnp.concatenate(blocks, axis=0)
    assert CSUM_ROWS % tt == 0
    tri = _tri(CSUM_ROWS // tt, tt)
    return jnp.concatenate([_dot(tri, pk[r0:r0 + CSUM_ROWS]) for r0 in range(0, m, CSUM_ROWS)], axis=0)


def _cumsum_cols(logf, carry, nb, tt, eq_ref, ek_ref, oq_ref, ok_ref):
    lane = lax.broadcasted_iota(jnp.int32, logf.shape, 1)
    head = lane < H
    lf = jnp.where(head, logf, 0.0)
    cs = _running_sum(_pack3(lf).astype(bf16), nb, tt)
    c = cs + pltpu.roll(cs, LANES - H, 1) + pltpu.roll(cs, LANES - 2 * H, 1)
    c = jnp.where(head, c, 0.0) + carry
    cp = _pack3(c * LOG2E).astype(bf16)
    qa = _dot(cp, eq_ref[...]) + oq_ref[...]
    ka = _dot(cp, ek_ref[...]) + ok_ref[...]
    return c, qa.astype(bf16), ka.astype(bf16)


def _tri(nb, tt):
    m = nb * tt
    r = lax.broadcasted_iota(jnp.int32, (m, m), 0)
    c = lax.broadcasted_iota(jnp.int32, (m, m), 1)
    if nb == 1:
        keep = c <= r
    else:
        assert tt & (tt - 1) == 0, "several streams per tile need a power-of-two tile length"
        sh = tt.bit_length() - 1
        keep = (c <= r) & ((r >> sh) == (c >> sh))
    return jnp.where(keep, 1.0, 0.0).astype(bf16)


def _aug_constants():
    eq = np.zeros((LANES, D), np.float32)
    ek = np.zeros((LANES, D), np.float32)
    oq = np.zeros((1, D), np.float32)
    ok = np.zeros((1, D), np.float32)
    for h in range(H):
        base = (h // 2) * LANES + (h % 2) * 2 * PIECES
        for x in range(PIECES):
            eq[x * H + h, base + x] = 1.0
            ok[0, base + x] = 1.0
            oq[0, base + PIECES + x] = 1.0
            ek[x * H + h, base + PIECES + x] = -1.0
    return jnp.asarray(eq, bf16), jnp.asarray(ek, bf16), jnp.asarray(oq), jnp.asarray(ok)


def _proj_kernel(x_ref, c0_ref, kvg_ref, qg_ref, wk_ref, wv_ref, wf_ref, bf_ref, wq_ref,
                 eq_ref, ek_ref, oq_ref, ok_ref,
                 k_ref, v_ref, lf_ref, qm_ref, qa_ref, km_ref, ka_ref, vm_ref, carry_ref, *, nb, tt, v_transposed):
    j = pl.program_id(1)
    m = nb * tt
    x = x_ref[...].reshape(m, D)
    xn = x * lax.rsqrt(jnp.mean(x * x, axis=-1, keepdims=True) + EPS)
    hk = (xn * kvg_ref[...]).astype(bf16)
    hq = (xn * qg_ref[...]).astype(bf16)

    zf = _dot(hk, wf_ref[...])
    k = _dot(hk, wk_ref[...])
    v = _dot(hk, wv_ref[...])
    q = _dot(hq, wq_ref[...])
    k_ref[...] = k.reshape(nb, tt, H, DH)
    v_ref[...] = v.reshape(nb, tt, H, DH)
    km_ref[...] = k.astype(bf16).reshape(nb, tt, D)
    if v_transposed:
        vm_ref[0] = v.T.astype(bf16)
    else:
        vm_ref[...] = v.astype(bf16).reshape(nb, tt, D)
    qm_ref[...] = q.astype(bf16).reshape(nb, tt, D)

    logf = jax.nn.log_sigmoid(zf + bf_ref[...])
    lf_ref[...] = logf[:, 0:H].reshape(nb, tt, H)

    @pl.when(j == 0)
    def _():
        carry_ref[...] = c0_ref[...]

    carry = jnp.broadcast_to(carry_ref[...], (nb, tt, LANES)).reshape(m, LANES)
    c, qa, ka = _cumsum_cols(logf, carry, nb, tt, eq_ref, ek_ref, oq_ref, ok_ref)
    qa_ref[...] = qa.reshape(nb, tt, D)
    ka_ref[...] = ka.reshape(nb, tt, D)
    carry_ref[...] = c.reshape(nb, tt, LANES)[:, tt - 1:tt, :]


def _proj(x, c0, kvg, qg, wk, wv, wf, bfp, wq, consts, *, nb, tt, v_transposed):
    b, t, _ = x.shape
    grid = (b // nb, t // tt)
    tile = pl.BlockSpec((nb, tt, D), lambda i, j: (i, j, 0))
    heads = pl.BlockSpec((nb, tt, H, DH), lambda i, j: (i, j, 0, 0))
    wide = jax.ShapeDtypeStruct((b, t, H, DH), f32)
    half = jax.ShapeDtypeStruct((b, t, D), bf16)
    if v_transposed:
        assert nb == 1
        vm_spec = pl.BlockSpec((1, D, tt), lambda i, j: (i, 0, j))
        vm_shape = jax.ShapeDtypeStruct((b, D, t), bf16)
    else:
        vm_spec, vm_shape = tile, half
    return pl.pallas_call(
        functools.partial(_proj_kernel, nb=nb, tt=tt, v_transposed=v_transposed),
        grid=grid,
        in_specs=[
            tile,
            pl.BlockSpec((nb, 1, LANES), lambda i, j: (i, 0, 0)),
            _const_spec((1, D)), _const_spec((1, D)),
            _const_spec((D, D)), _const_spec((D, D)), _const_spec((D, LANES)), _const_spec((1, LANES)),
            _const_spec((D, D)),
            _const_spec((LANES, D)), _const_spec((LANES, D)), _const_spec((1, D)), _const_spec((1, D)),
        ],
        out_specs=[heads, heads, pl.BlockSpec((nb, tt, H), lambda i, j: (i, j, 0)), tile, tile, tile, tile, vm_spec],
        out_shape=[wide, wide, jax.ShapeDtypeStruct((b, t, H), f32), half, half, half, half, vm_shape],
        scratch_shapes=[pltpu.VMEM((nb, 1, LANES), f32)],
        compiler_params=_params(("parallel", "arbitrary")),
        name="kvq_proj",
    )(x, c0, kvg, qg, wk, wv, wf, bfp, wq, *consts)


def _cache_aug_kernel(lf_ref, place_ref, eq_ref, ek_ref, oq_ref, ok_ref, ka_ref, ctot_ref, *, p):
    lf16 = lf_ref[0]
    hi, mid, lo = _split3(lf16)
    place = place_ref[...]
    logf = _dot(hi.astype(bf16), place) + _dot(mid.astype(bf16), place) + _dot(lo.astype(bf16), place)
    c, _, ka = _cumsum_cols(logf, jnp.zeros((1, LANES), f32), 1, p, eq_ref, ek_ref, oq_ref, ok_ref)
    ka_ref[0] = ka
    ctot_ref[0] = c[p - 1:p, :]


def _cache_aug(cache_logf, consts):
    b, p, _ = cache_logf.shape
    place = jnp.asarray(np.eye(H, LANES, dtype=np.float32), bf16)
    return pl.pallas_call(
        functools.partial(_cache_aug_kernel, p=p),
        grid=(b,),
        in_specs=[
            pl.BlockSpec((1, p, H), lambda i: (i, 0, 0)),
            _const_spec((H, LANES)),
            _const_spec((LANES, D)), _const_spec((LANES, D)), _const_spec((1, D)), _const_spec((1, D)),
        ],
        out_specs=[pl.BlockSpec((1, p, D), lambda i: (i, 0, 0)), pl.BlockSpec((1, 1, LANES), lambda i: (i, 0, 0))],
        out_shape=[jax.ShapeDtypeStruct((b, p, D), bf16), jax.ShapeDtypeStruct((b, 1, LANES), f32)],
        compiler_params=_params(("parallel",)),
        name="cache_aug",
    )(cache_logf, place, *consts)


def _stack_heads(qm, qa):
    t = qm.shape[0]
    q2 = jnp.concatenate([qm, qa], axis=1).astype(f32)
    lane = lax.broadcasted_iota(jnp.int32, (t, 2 * LANES), 1)
    grp = 2 * PIECES
    keep_a = (lane < DH) | ((lane >= LANES) & (lane < LANES + grp))
    keep_b = ((lane >= DH) & (lane < LANES)) | ((lane >= LANES + grp) & (lane < LANES + 2 * grp))
    return jnp.concatenate([jnp.where(keep_a, q2, 0.0), jnp.where(keep_b, q2, 0.0)], axis=0).astype(bf16)


def _unstack_heads(o, t):
    lane = lax.broadcasted_iota(jnp.int32, (t, LANES), 1)
    return jnp.where(lane < DH, o[0:t], o[t:2 * t])


def _causal(s, t):
    row = lax.broadcasted_iota(jnp.int32, s.shape, 0)
    col = lax.broadcasted_iota(jnp.int32, s.shape, 1)
    row = jnp.where(row >= t, row - t, row)
    return jnp.where(col <= row, s, NEG)


QGRP = 256


AHEAD = 2


def _attn_prompt_kernel(qm_ref, qa_ref, km_ref, ka_ref, vt_ref, o_ref,
                        qs_ref, m_ref, l_ref, acc_ref, s_ref, p_ref, a_ref, *, tq):
    i = pl.program_id(2)
    ngrp = 2 * tq // QGRP
    qs_ref[...] = _stack_heads(qm_ref[0], qa_ref[0])
    m_ref[...] = jnp.full(m_ref.shape, NEG, f32)
    l_ref[...] = jnp.zeros(l_ref.shape, f32)
    acc_ref[...] = jnp.zeros(acc_ref.shape, f32)
    p_ref[...] = jnp.zeros(p_ref.shape, bf16)
    a_ref[...] = jnp.ones(a_ref.shape, f32)

    def keys(jb):
        off = pl.multiple_of(jb * tq, tq)
        return jnp.concatenate([km_ref[0, pl.ds(off, tq), :], ka_ref[0, pl.ds(off, tq), :]], axis=1)

    def values(jb):
        vt = vt_ref[0, :, pl.ds(pl.multiple_of(jb * tq, tq), tq)]
        ones = jnp.ones((ONES_ROWS, tq), bf16)
        return [jnp.concatenate([vt[h * DH:(h + 1) * DH, :], ones], axis=0) for h in range(2)]

    def scores(kblk, g):
        return _dot_nt(kblk, qs_ref[g * QGRP:(g + 1) * QGRP, :])

    def weighted_values(vt_aug, p, alpha, g):
        c0 = g * QGRP
        head, q0 = c0 // tq, c0 % tq
        rows = slice(head * DH, (head + 1) * DH)
        r = _dot(vt_aug[head], p)
        acc_ref[rows, q0:q0 + QGRP] = alpha * acc_ref[rows, q0:q0 + QGRP] + r[0:DH]
        l_ref[:, c0:c0 + QGRP] = alpha * l_ref[:, c0:c0 + QGRP] + r[DH:DH + 1]

    def softmax(s, g, masked):
        c0 = g * QGRP
        if masked:
            key = lax.broadcasted_iota(jnp.int32, s.shape, 0)
            qry = lax.broadcasted_iota(jnp.int32, s.shape, 1) + c0 % tq
            s = jnp.where(key <= qry, s, NEG)
        m_prev = m_ref[:, c0:c0 + QGRP]
        m_new = jnp.maximum(m_prev, jnp.max(s, axis=0, keepdims=True))
        m_ref[:, c0:c0 + QGRP] = m_new
        return jnp.exp2(s - m_new).astype(bf16), jnp.exp2(m_prev - m_new)

    def run(jbs, masked, nxt):
        kb = [keys(jb) for jb in jbs]
        vb = [values(jb) for jb in jbs]
        kn = None if nxt is None else keys(nxt)
        items = [(bi, g) for bi in range(len(jbs)) for g in range(ngrp)]
        ready = {}
        prev = (values(jnp.maximum(jbs[0] - 1, 0)), p_ref[...], a_ref[...], ngrp - 1)
        for t, (bi, g) in enumerate(items):
            s = s_ref[t] if t < AHEAD else ready.pop(t)
            ta = t + AHEAD
            if ta < len(items):
                ready[ta] = scores(kb[items[ta][0]], items[ta][1])
            elif kn is not None:
                s_ref[ta - len(items)] = scores(kn, ta - len(items))
            weighted_values(*prev)
            p, alpha = softmax(s, g, masked)
            prev = (vb[bi], p, alpha, g)
        if nxt is None:
            weighted_values(*prev)
        else:
            p_ref[...] = prev[1]
            a_ref[...] = prev[2]

    def pair_body(t, carry):
        run([2 * t, 2 * t + 1], False, 2 * t + 2)
        return carry

    first = keys(0)
    for g in range(AHEAD):
        s_ref[g] = scores(first, g)
    lax.fori_loop(0, i >> 1, pair_body, 0)

    @pl.when((i & 1) == 1)
    def _():
        run([i - 1], False, i)

    run([i], True, None)
    inv = 1.0 / l_ref[...]
    ot = jnp.concatenate([acc_ref[0:DH, :] * inv[:, 0:tq], acc_ref[DH:2 * DH, :] * inv[:, tq:2 * tq]], axis=0)
    o_ref[0] = ot.T.astype(bf16)


def _attn_prompt(qm, qa, km, ka, vt, *, tq):
    b, t, _ = qm.shape
    assert tq % QGRP == 0 and 2 * tq // QGRP > AHEAD
    qspec = pl.BlockSpec((1, tq, LANES), lambda bi, p, i: (bi, i, p))
    kspec = pl.BlockSpec((1, t, LANES), lambda bi, p, i: (bi, 0, p))
    vspec = pl.BlockSpec((1, LANES, t), lambda bi, p, i: (bi, p, 0))
    return pl.pallas_call(
        functools.partial(_attn_prompt_kernel, tq=tq),
        grid=(b, NPAIR, t // tq),
        in_specs=[qspec, qspec, kspec, kspec, vspec],
        out_specs=qspec,
        out_shape=jax.ShapeDtypeStruct((b, t, D), bf16),
        scratch_shapes=[
            pltpu.VMEM((2 * tq, 2 * LANES), bf16),
            pltpu.VMEM((1, 2 * tq), f32),
            pltpu.VMEM((1, 2 * tq), f32),
            pltpu.VMEM((LANES, tq), f32),
            pltpu.VMEM((AHEAD, tq, QGRP), f32),
            pltpu.VMEM((tq, QGRP), bf16),
            pltpu.VMEM((1, QGRP), f32),
        ],
        compiler_params=_params(("parallel", "parallel", "arbitrary")),
        name="fox_attn_prompt",
    )(qm, qa, km, ka, vt)


def _attn_sample_kernel(qm_ref, qa_ref, kc_ref, kca_ref, vc_ref, kn_ref, kna_ref, vn_ref, o_ref, *, t):
    scores = []
    for pr in range(NPAIR):
        sl = slice(pr * LANES, (pr + 1) * LANES)
        qs = _stack_heads(qm_ref[0, :, sl], qa_ref[0, :, sl])
        kc = jnp.concatenate([kc_ref[0, :, sl].astype(bf16), kca_ref[0, :, sl]], axis=1)
        kn = jnp.concatenate([kn_ref[0, :, sl], kna_ref[0, :, sl]], axis=1)
        scores.append((_dot_nt(qs, kc), _dot_nt(qs, kn)))
    for pr, (s_c, s_n) in enumerate(scores):
        sl = slice(pr * LANES, (pr + 1) * LANES)
        s_n = _causal(s_n, t)
        m = jnp.maximum(jnp.max(s_c, axis=1, keepdims=True), jnp.max(s_n, axis=1, keepdims=True))
        p_c = jnp.exp2(s_c - m)
        p_n = jnp.exp2(s_n - m)
        l = jnp.sum(p_c, axis=1, keepdims=True) + jnp.sum(p_n, axis=1, keepdims=True)
        o = _dot(p_c.astype(bf16), vc_ref[0, :, sl].astype(bf16)) + _dot(p_n.astype(bf16), vn_ref[0, :, sl])
        o_ref[0, :, sl] = _unstack_heads(o * (1.0 / l), t).astype(bf16)


def _attn_sample(qm, qa, kc, kca, vc, kn, kna, vn):
    b, t, _ = qm.shape
    p = kc.shape[1]
    new = pl.BlockSpec((1, t, D), lambda bi: (bi, 0, 0))
    old = pl.BlockSpec((1, p, D), lambda bi: (bi, 0, 0))
    return pl.pallas_call(
        functools.partial(_attn_sample_kernel, t=t),
        grid=(b,),
        in_specs=[new, new, old, old, old, new, new, new],
        out_specs=new,
        out_shape=jax.ShapeDtypeStruct((b, t, D), bf16),
        compiler_params=_params(("parallel",)),
        name="fox_attn_sample",
    )(qm, qa, kc, kca, vc, kn, kna, vn)


def _trunk(x, sa, sf, past, w, consts, *, nb, tt, tq):
    b, t, _ = x.shape
    x1, nsa = _mixer(x, sa[0], w["a_g"], w["w_in"], w["a_cw"], w["w_out"], nb=nb, tt=tt)
    x2, nsf0 = _ffn(x1, sf[0], w["f_g"][0], w["w_up"][0], w["f_cw"][0], w["w_down"][0], nb=nb, tt=tt)
    if past is None:
        c0 = jnp.zeros((b, 1, LANES), f32)
    else:
        cache_k, cache_v, cache_logf = past
        kca, c0 = _cache_aug(cache_logf, consts)
    k, v, logf, qm, qa, km, ka, vm = _proj(x2, c0, w["kv_g"], w["q_g"], w["w_k"], w["w_v"], w["w_f"], w["b_f"],
                                           w["w_q"], consts, nb=nb, tt=tt, v_transposed=past is None)
    if past is None:
        o = _attn_prompt(qm, qa, km, ka, vm, tq=tq)
    else:
        p = cache_k.shape[1]
        o = _attn_sample(qm, qa, cache_k.reshape(b, p, D), kca, cache_v.reshape(b, p, D), km, ka, vm)
    y, nsf1 = _ffn(x2, sf[1], w["f_g"][1], w["w_up"][1], w["f_cw"][1], w["w_down"][1], nb=nb, tt=tt,
                   attn=(o, w["w_o"]), final_g=w["final_g"])
    return (y, nsa[None], jnp.stack([nsf0, nsf1]), k, v, logf)


def kernel(x_prompt, x_sample, state_conv_a, state_ffn_conv, cache_k, cache_v, cache_logf, a_norm, w_a_in,
           a_conv_w, w_a_out, kv_norm, w_kv, b_f, b_norm, w_q, w_o, ffn_norm, w_ffn_up, ffn_conv_w,
           w_ffn_down, final_norm):
    assert a_norm.shape[0] == 1 and b_norm.shape[0] == 1 and ffn_norm.shape[0] == 2
    w = {
        "a_g": a_norm[0][None], "w_in": w_a_in[0].astype(bf16), "a_cw": a_conv_w[0],
        "w_out": w_a_out[0].astype(bf16),
        "f_g": [ffn_norm[l][None] for l in range(2)],
        "w_up": [w_ffn_up[l].astype(bf16) for l in range(2)],
        "f_cw": [ffn_conv_w[l] for l in range(2)],
        "w_down": [w_ffn_down[l].astype(bf16) for l in range(2)],
        "kv_g": kv_norm[None], "q_g": b_norm[0][None],
        "w_k": w_kv[:, 0:D].astype(bf16), "w_v": w_kv[:, D:2 * D].astype(bf16),
        "w_f": jnp.pad(w_kv[:, 2 * D:], ((0, 0), (0, LANES - H))).astype(bf16),
        "b_f": jnp.pad(b_f, (0, LANES - H))[None],
        "w_q": (w_q[0] * (DH ** -0.5 * LOG2E)).astype(bf16),
        "w_o": w_o[0].astype(bf16), "final_g": final_norm[None],
    }
    consts = _aug_constants()
    bp = x_prompt.shape[0]
    sa0 = jnp.zeros((1, bp, 2, D), f32)
    sf0 = jnp.zeros((2, bp, 2, 2 * F), f32)
    outs_p = _trunk(x_prompt, sa0, sf0, None, w, consts, nb=1, tt=512, tq=512)
    outs_s = _trunk(x_sample, state_conv_a, state_ffn_conv, (cache_k, cache_v, cache_logf), w, consts,
                    nb=16, tt=x_sample.shape[1], tq=None)
    y_p, p_a, p_f, p_k, p_v, p_lf = outs_p
    y_s, s_a, s_f, s_k, s_v, s_lf = outs_s
    return (y_p, y_s, p_a, p_f, p_k, p_v, p_lf, s_a, s_f, s_k, s_v, s_lf)
```

```python
import functools

import numpy as np
import jax
import jax.numpy as jnp
from jax import lax
from jax.experimental import pallas as pl
from jax.experimental.pallas import tpu as pltpu

D = 1024
F = 2816
H = 16
DH = 64
NPAIR = H // 2
LANES = 128
CH = 256
NCH = F // CH
EPS = 1e-6
NEG = -1e30
PIECES = 3
LOG2E = 1.4426950408889634
ONES_ROWS = 16
VMEM_LIMIT = 56 * 1024 * 1024

f32 = jnp.float32
bf16 = jnp.bfloat16


def _const_spec(shape):
    zeros = (0,) * len(shape)
    return pl.BlockSpec(shape, lambda *_: zeros, pipeline_mode=pl.Buffered(1))


def _params(sem):
    return pltpu.CompilerParams(dimension_semantics=sem, vmem_limit_bytes=VMEM_LIMIT)


def _rms(x, g):
    ms = jnp.mean(x * x, axis=-1, keepdims=True)
    return (x * lax.rsqrt(ms + EPS)) * g


def _dot(a, b):
    return jnp.dot(a, b, preferred_element_type=f32)


def _dot_nt(a, b):
    return lax.dot_general(a, b, (((1,), (1,)), ((), ())), preferred_element_type=f32)


HIST = 8


def _init_history(st_ref, ctx_ref):
    st_ref[...] = jnp.zeros(st_ref.shape, f32)
    st_ref[:, HIST - 2:HIST, :] = ctx_ref[...]


def _causal_conv(u, st_ref, cw_ref, lo, nb, tt):
    hi = lo + CH
    u3 = u.reshape(nb, tt, CH)
    rows = nb * (HIST + tt)
    xp = jnp.concatenate([st_ref[:, :, lo:hi], u3], axis=1).reshape(rows, CH)
    cw = cw_ref[:, lo:hi]
    y = pltpu.roll(xp, 2, 0) * cw[0:1] + pltpu.roll(xp, 1, 0) * cw[1:2]
    st_ref[:, HIST - 2:HIST, lo:hi] = u3[:, tt - 2:tt, :]
    return y.reshape(nb, HIST + tt, CH)[:, HIST:, :].reshape(nb * tt, CH) + u * cw[2:3]


def _mix_kernel(x_ref, ctx_ref, g_ref, win_ref, cw_ref, wout_ref, y_ref, nctx_ref, st_ref, zg_ref, *, nb, tt):
    j = pl.program_id(1)
    m = nb * tt
    x = x_ref[...].reshape(m, D)
    h = _rms(x, g_ref[...]).astype(bf16)

    @pl.when(j == 0)
    def _():
        _init_history(st_ref, ctx_ref)

    def up(c):
        lo = c * CH
        return tuple(_dot(h, win_ref[:, part * D + lo:part * D + lo + CH]) for part in range(3))

    nch = D // CH
    y = x
    nxt = up(0)
    for c in range(nch):
        lo = c * CH
        gb, gc, u = nxt
        if c + 1 < nch:
            nxt = up(c + 1)
        z = _causal_conv(gc * u, st_ref, cw_ref, lo, nb, tt)
        zg_ref[:, lo:lo + CH] = (gb * z).astype(bf16)
        if c == nch // 2 - 1:
            y = y + _dot(zg_ref[:, 0:lo + CH], wout_ref[0:lo + CH, :])
    y = y + _dot(zg_ref[:, D // 2:D], wout_ref[D // 2:D, :])
    y_ref[...] = y.reshape(nb, tt, D)
    nctx_ref[...] = st_ref[:, HIST - 2:HIST, :]


def _mixer(x, ctx, g, w_in, cw, w_out, *, nb, tt):
    b, t, _ = x.shape
    grid = (b // nb, t // tt)
    return pl.pallas_call(
        functools.partial(_mix_kernel, nb=nb, tt=tt),
        grid=grid,
        in_specs=[
            pl.BlockSpec((nb, tt, D), lambda i, j: (i, j, 0)),
            pl.BlockSpec((nb, 2, D), lambda i, j: (i, 0, 0)),
            _const_spec((1, D)),
            _const_spec((D, 3 * D)),
            _const_spec((3, D)),
            _const_spec((D, D)),
        ],
        out_specs=[
            pl.BlockSpec((nb, tt, D), lambda i, j: (i, j, 0)),
            pl.BlockSpec((nb, 2, D), lambda i, j: (i, 0, 0)),
        ],
        out_shape=[jax.ShapeDtypeStruct((b, t, D), f32), jax.ShapeDtypeStruct((b, 2, D), f32)],
        scratch_shapes=[pltpu.VMEM((nb, HIST, D), f32), pltpu.VMEM((nb * tt, D), bf16)],
        compiler_params=_params(("parallel", "arbitrary")),
        name="mixer_a",
    )(x, ctx, g, w_in, cw, w_out)


def _ffn_kernel(*refs, nb, tt, with_attn, with_final):
    refs = list(refs)
    x_ref = refs.pop(0)
    if with_attn:
        o_ref, wo_ref = refs.pop(0), refs.pop(0)
    ctx_ref, g_ref, wup_ref, cw_ref, wdown_ref = (refs.pop(0) for _ in range(5))
    if with_final:
        fg_ref = refs.pop(0)
    y_ref, nctx_ref, act_ref, st_ref = refs

    j = pl.program_id(1)
    m = nb * tt
    x = x_ref[...].reshape(m, D)
    if with_attn:
        x = x + _dot(o_ref[...].reshape(m, D), wo_ref[...])
    h = _rms(x, g_ref[...]).astype(bf16)

    @pl.when(j == 0)
    def _():
        _init_history(st_ref, ctx_ref)

    def up(c):
        glo, vlo = c * CH, F + c * CH
        return _dot(h, wup_ref[:, glo:glo + CH]), _dot(h, wup_ref[:, vlo:vlo + CH])

    split = (NCH + 1) // 2 * CH
    y = x
    nxt = up(0)
    for c in range(NCH):
        glo, vlo = c * CH, F + c * CH
        ug, uv = nxt
        if c + 1 < NCH:
            nxt = up(c + 1)
        gate = _causal_conv(ug, st_ref, cw_ref, glo, nb, tt)
        val = _causal_conv(uv, st_ref, cw_ref, vlo, nb, tt)
        act_ref[:, glo:glo + CH] = (gate * (1.0 / (1.0 + jnp.exp(-gate))) * val).astype(bf16)
        if glo + CH == split:
            y = y + _dot(act_ref[:, 0:split], wdown_ref[0:split, :])
    y = y + _dot(act_ref[:, split:F], wdown_ref[split:F, :])
    if with_final:
        y = _rms(y, fg_ref[...])
    y_ref[...] = y.reshape(nb, tt, D)
    nctx_ref[...] = st_ref[:, HIST - 2:HIST, :]


def _ffn(x, ctx, g, w_up, cw, w_down, *, nb, tt, attn=None, final_g=None):
    b, t, _ = x.shape
    grid = (b // nb, t // tt)
    tile = pl.BlockSpec((nb, tt, D), lambda i, j: (i, j, 0))
    args, specs = [x], [tile]
    if attn is not None:
        o, w_o = attn
        args += [o, w_o]
        specs += [tile, _const_spec((D, D))]
    args += [ctx, g, w_up, cw, w_down]
    specs += [
        pl.BlockSpec((nb, 2, 2 * F), lambda i, j: (i, 0, 0)),
        _const_spec((1, D)),
        _const_spec((D, 2 * F)),
        _const_spec((3, 2 * F)),
        _const_spec((F, D)),
    ]
    if final_g is not None:
        args.append(final_g)
        specs.append(_const_spec((1, D)))
    return pl.pallas_call(
        functools.partial(_ffn_kernel, nb=nb, tt=tt, with_attn=attn is not None, with_final=final_g is not None),
        grid=grid,
        in_specs=specs,
        out_specs=[tile, pl.BlockSpec((nb, 2, 2 * F), lambda i, j: (i, 0, 0))],
        out_shape=[jax.ShapeDtypeStruct((b, t, D), f32), jax.ShapeDtypeStruct((b, 2, 2 * F), f32)],
        scratch_shapes=[
            pltpu.VMEM((nb * tt, F), bf16),
            pltpu.VMEM((nb, HIST, 2 * F), f32),
        ],
        compiler_params=_params(("parallel", "arbitrary")),
        name="conv_ffn_attn" if attn is not None else "conv_ffn",
    )(*args)


def _split3(c):
    hi = c.astype(bf16).astype(f32)
    r = c - hi
    mid = r.astype(bf16).astype(f32)
    lo = (r - mid).astype(bf16).astype(f32)
    return hi, mid, lo


def _pack3(c):
    hi, mid, lo = _split3(c)
    return hi + pltpu.roll(mid, H, 1) + pltpu.roll(lo, 2 * H, 1)


CSUM_ROWS = 256


def _running_sum(pk, nb, tt):
    m = nb * tt
    if m <= CSUM_ROWS:
        return _dot(_tri(nb, tt), pk)
    assert m % CSUM_ROWS == 0
    if nb == 1:
        tri = _tri(1, CSUM_ROWS)
        blocks, run = [], jnp.zeros((1, LANES), f32)
        for r0 in range(0, m, CSUM_ROWS):
            blk = _dot(tri, pk[r0:r0 + CSUM_ROWS]) + run
            blocks.append(blk)
            run = blk[CSUM_ROWS - 1:CSUM_ROWS]
        return jnp.concatenate(blocks, axis=0)
    assert CSUM_ROWS % tt == 0
    tri = _tri(CSUM_ROWS // tt, tt)
    return jnp.concatenate([_dot(tri, pk[r0:r0 + CSUM_ROWS]) for r0 in range(0, m, CSUM_ROWS)], axis=0)


def _cumsum_cols(logf, carry, nb, tt, eq_ref, ek_ref, oq_ref, ok_ref):
    lane = lax.broadcasted_iota(jnp.int32, logf.shape, 1)
    head = lane < H
    lf = jnp.where(head, logf, 0.0)
    cs = _running_sum(_pack3(lf).astype(bf16), nb, tt)
    c = cs + pltpu.roll(cs, LANES - H, 1) + pltpu.roll(cs, LANES - 2 * H, 1)
    c = jnp.where(head, c, 0.0) + carry
    cp = _pack3(c * LOG2E).astype(bf16)
    qa = _dot(cp, eq_ref[...]) + oq_ref[...]
    ka = _dot(cp, ek_ref[...]) + ok_ref[...]
    return c, qa.astype(bf16), ka.astype(bf16)


def _tri(nb, tt):
    m = nb * tt
    r = lax.broadcasted_iota(jnp.int32, (m, m), 0)
    c = lax.broadcasted_iota(jnp.int32, (m, m), 1)
    if nb == 1:
        keep = c <= r
    else:
        assert tt & (tt - 1) == 0, "several streams per tile need a power-of-two tile length"
        sh = tt.bit_length() - 1
        keep = (c <= r) & ((r >> sh) == (c >> sh))
    return jnp.where(keep, 1.0, 0.0).astype(bf16)


def _aug_constants():
    eq = np.zeros((LANES, D), np.float32)
    ek = np.zeros((LANES, D), np.float32)
    oq = np.zeros((1, D), np.float32)
    ok = np.zeros((1, D), np.float32)
    for h in range(H):
        base = (h // 2) * LANES + (h % 2) * 2 * PIECES
        for x in range(PIECES):
            eq[x * H + h, base + x] = 1.0
            ok[0, base + x] = 1.0
            oq[0, base + PIECES + x] = 1.0
            ek[x * H + h, base + PIECES + x] = -1.0
    return jnp.asarray(eq, bf16), jnp.asarray(ek, bf16), jnp.asarray(oq), jnp.asarray(ok)


def _proj_kernel(x_ref, c0_ref, kvg_ref, qg_ref, wk_ref, wv_ref, wf_ref, bf_ref, wq_ref,
                 eq_ref, ek_ref, oq_ref, ok_ref,
                 k_ref, v_ref, lf_ref, qm_ref, qa_ref, km_ref, ka_ref, vm_ref, carry_ref, *, nb, tt, v_transposed):
    j = pl.program_id(1)
    m = nb * tt
    x = x_ref[...].reshape(m, D)
    xn = x * lax.rsqrt(jnp.mean(x * x, axis=-1, keepdims=True) + EPS)
    hk = (xn * kvg_ref[...]).astype(bf16)
    hq = (xn * qg_ref[...]).astype(bf16)

    zf = _dot(hk, wf_ref[...])
    k = _dot(hk, wk_ref[...])
    v = _dot(hk, wv_ref[...])
    q = _dot(hq, wq_ref[...])
    k_ref[...] = k.reshape(nb, tt, H, DH)
    v_ref[...] = v.reshape(nb, tt, H, DH)
    km_ref[...] = k.astype(bf16).reshape(nb, tt, D)
    if v_transposed:
        vm_ref[0] = v.T.astype(bf16)
    else:
        vm_ref[...] = v.astype(bf16).reshape(nb, tt, D)
    qm_ref[...] = q.astype(bf16).reshape(nb, tt, D)

    logf = jax.nn.log_sigmoid(zf + bf_ref[...])
    lf_ref[...] = logf[:, 0:H].reshape(nb, tt, H)

    @pl.when(j == 0)
    def _():
        carry_ref[...] = c0_ref[...]

    carry = jnp.broadcast_to(carry_ref[...], (nb, tt, LANES)).reshape(m, LANES)
    c, qa, ka = _cumsum_cols(logf, carry, nb, tt, eq_ref, ek_ref, oq_ref, ok_ref)
    qa_ref[...] = qa.reshape(nb, tt, D)
    ka_ref[...] = ka.reshape(nb, tt, D)
    carry_ref[...] = c.reshape(nb, tt, LANES)[:, tt - 1:tt, :]


def _proj(x, c0, kvg, qg, wk, wv, wf, bfp, wq, consts, *, nb, tt, v_transposed):
    b, t, _ = x.shape
    grid = (b // nb, t // tt)
    tile = pl.BlockSpec((nb, tt, D), lambda i, j: (i, j, 0))
    heads = pl.BlockSpec((nb, tt, H, DH), lambda i, j: (i, j, 0, 0))
    wide = jax.ShapeDtypeStruct((b, t, H, DH), f32)
    half = jax.ShapeDtypeStruct((b, t, D), bf16)
    if v_transposed:
        assert nb == 1
        vm_spec = pl.BlockSpec((1, D, tt), lambda i, j: (i, 0, j))
        vm_shape = jax.ShapeDtypeStruct((b, D, t), bf16)
    else:
        vm_spec, vm_shape = tile, half
    return pl.pallas_call(
        functools.partial(_proj_kernel, nb=nb, tt=tt, v_transposed=v_transposed),
        grid=grid,
        in_specs=[
            tile,
            pl.BlockSpec((nb, 1, LANES), lambda i, j: (i, 0, 0)),
            _const_spec((1, D)), _const_spec((1, D)),
            _const_spec((D, D)), _const_spec((D, D)), _const_spec((D, LANES)), _const_spec((1, LANES)),
            _const_spec((D, D)),
            _const_spec((LANES, D)), _const_spec((LANES, D)), _const_spec((1, D)), _const_spec((1, D)),
        ],
        out_specs=[heads, heads, pl.BlockSpec((nb, tt, H), lambda i, j: (i, j, 0)), tile, tile, tile, tile, vm_spec],
        out_shape=[wide, wide, jax.ShapeDtypeStruct((b, t, H), f32), half, half, half, half, vm_shape],
        scratch_shapes=[pltpu.VMEM((nb, 1, LANES), f32)],
        compiler_params=_params(("parallel", "arbitrary")),
        name="kvq_proj",
    )(x, c0, kvg, qg, wk, wv, wf, bfp, wq, *consts)


def _cache_aug_kernel(lf_ref, place_ref, eq_ref, ek_ref, oq_ref, ok_ref, ka_ref, ctot_ref, *, p):
    lf16 = lf_ref[0]
    hi, mid, lo = _split3(lf16)
    place = place_ref[...]
    logf = _dot(hi.astype(bf16), place) + _dot(mid.astype(bf16), place) + _dot(lo.astype(bf16), place)
    c, _, ka = _cumsum_cols(logf, jnp.zeros((1, LANES), f32), 1, p, eq_ref, ek_ref, oq_ref, ok_ref)
    ka_ref[0] = ka
    ctot_ref[0] = c[p - 1:p, :]


def _cache_aug(cache_logf, consts):
    b, p, _ = cache_logf.shape
    place = jnp.asarray(np.eye(H, LANES, dtype=np.float32), bf16)
    return pl.pallas_call(
        functools.partial(_cache_aug_kernel, p=p),
        grid=(b,),
        in_specs=[
            pl.BlockSpec((1, p, H), lambda i: (i, 0, 0)),
            _const_spec((H, LANES)),
            _const_spec((LANES, D)), _const_spec((LANES, D)), _const_spec((1, D)), _const_spec((1, D)),
        ],
        out_specs=[pl.BlockSpec((1, p, D), lambda i: (i, 0, 0)), pl.BlockSpec((1, 1, LANES), lambda i: (i, 0, 0))],
        out_shape=[jax.ShapeDtypeStruct((b, p, D), bf16), jax.ShapeDtypeStruct((b, 1, LANES), f32)],
        compiler_params=_params(("parallel",)),
        name="cache_aug",
    )(cache_logf, place, *consts)


def _stack_heads(qm, qa):
    t = qm.shape[0]
    q2 = jnp.concatenate([qm, qa], axis=1).astype(f32)
    lane = lax.broadcasted_iota(jnp.int32, (t, 2 * LANES), 1)
    grp = 2 * PIECES
    keep_a = (lane < DH) | ((lane >= LANES) & (lane < LANES + grp))
    keep_b = ((lane >= DH) & (lane < LANES)) | ((lane >= LANES + grp) & (lane < LANES + 2 * grp))
    return jnp.concatenate([jnp.where(keep_a, q2, 0.0), jnp.where(keep_b, q2, 0.0)], axis=0).astype(bf16)


def _unstack_heads(o, t):
    lane = lax.broadcasted_iota(jnp.int32, (t, LANES), 1)
    return jnp.where(lane < DH, o[0:t], o[t:2 * t])


def _causal(s, t):
    row = lax.broadcasted_iota(jnp.int32, s.shape, 0)
    col = lax.broadcasted_iota(jnp.int32, s.shape, 1)
    row = jnp.where(row >= t, row - t, row)
    return jnp.where(col <= row, s, NEG)


QGRP = 256


AHEAD = 3


def _attn_prompt_kernel(qm_ref, qa_ref, km_ref, ka_ref, vt_ref, o_ref,
                        qs_ref, m_ref, l_ref, acc_ref, s_ref, p_ref, a_ref, *, tq):
    i = pl.program_id(2)
    ngrp = 2 * tq // QGRP
    qs_ref[...] = _stack_heads(qm_ref[0], qa_ref[0])
    m_ref[...] = jnp.full(m_ref.shape, NEG, f32)
    l_ref[...] = jnp.zeros(l_ref.shape, f32)
    acc_ref[...] = jnp.zeros(acc_ref.shape, f32)
    p_ref[...] = jnp.zeros(p_ref.shape, bf16)
    a_ref[...] = jnp.ones(a_ref.shape, f32)

    def keys(jb):
        off = pl.multiple_of(jb * tq, tq)
        return jnp.concatenate([km_ref[0, pl.ds(off, tq), :], ka_ref[0, pl.ds(off, tq), :]], axis=1)

    def values(jb):
        vt = vt_ref[0, :, pl.ds(pl.multiple_of(jb * tq, tq), tq)]
        ones = jnp.ones((ONES_ROWS, tq), bf16)
        return [jnp.concatenate([vt[h * DH:(h + 1) * DH, :], ones], axis=0) for h in range(2)]

    def scores(kblk, g):
        return _dot_nt(kblk, qs_ref[g * QGRP:(g + 1) * QGRP, :])

    def weighted_values(vt_aug, p, alpha, g):
        c0 = g * QGRP
        head, q0 = c0 // tq, c0 % tq
        rows = slice(head * DH, (head + 1) * DH)
        r = _dot(vt_aug[head], p)
        acc_ref[rows, q0:q0 + QGRP] = alpha * acc_ref[rows, q0:q0 + QGRP] + r[0:DH]
        l_ref[:, c0:c0 + QGRP] = alpha * l_ref[:, c0:c0 + QGRP] + r[DH:DH + 1]

    def softmax(s, g, masked):
        c0 = g * QGRP
        if masked:
            key = lax.broadcasted_iota(jnp.int32, s.shape, 0)
            qry = lax.broadcasted_iota(jnp.int32, s.shape, 1) + c0 % tq
            s = jnp.where(key <= qry, s, NEG)
        m_prev = m_ref[:, c0:c0 + QGRP]
        m_new = jnp.maximum(m_prev, jnp.max(s, axis=0, keepdims=True))
        m_ref[:, c0:c0 + QGRP] = m_new
        return jnp.exp2(s - m_new).astype(bf16), jnp.exp2(m_prev - m_new)

    def run(jbs, masked, nxt):
        kb = [keys(jb) for jb in jbs]
        vb = [values(jb) for jb in jbs]
        kn = None if nxt is None else keys(nxt)
        items = [(bi, g) for bi in range(len(jbs)) for g in range(ngrp)]
        ready = {}
        prev = (values(jnp.maximum(jbs[0] - 1, 0)), p_ref[...], a_ref[...], ngrp - 1)
        for t, (bi, g) in enumerate(items):
            s = s_ref[t] if t < AHEAD else ready.pop(t)
            ta = t + AHEAD
            if ta < len(items):
                ready[ta] = scores(kb[items[ta][0]], items[ta][1])
            elif kn is not None:
                s_ref[ta - len(items)] = scores(kn, ta - len(items))
            weighted_values(*prev)
            p, alpha = softmax(s, g, masked)
            prev = (vb[bi], p, alpha, g)
        if nxt is None:
            weighted_values(*prev)
        else:
            p_ref[...] = prev[1]
            a_ref[...] = prev[2]

    def pair_body(t, carry):
        run([2 * t, 2 * t + 1], False, 2 * t + 2)
        return carry

    first = keys(0)
    for g in range(AHEAD):
        s_ref[g] = scores(first, g)
    lax.fori_loop(0, i >> 1, pair_body, 0)

    @pl.when((i & 1) == 1)
    def _():
        run([i - 1], False, i)

    run([i], True, None)
    inv = 1.0 / l_ref[...]
    ot = jnp.concatenate([acc_ref[0:DH, :] * inv[:, 0:tq], acc_ref[DH:2 * DH, :] * inv[:, tq:2 * tq]], axis=0)
    o_ref[0] = ot.T.astype(bf16)


def _attn_prompt(qm, qa, km, ka, vt, *, tq):
    b, t, _ = qm.shape
    assert tq % QGRP == 0 and 2 * tq // QGRP > AHEAD
    qspec = pl.BlockSpec((1, tq, LANES), lambda bi, p, i: (bi, i, p))
    kspec = pl.BlockSpec((1, t, LANES), lambda bi, p, i: (bi, 0, p))
    vspec = pl.BlockSpec((1, LANES, t), lambda bi, p, i: (bi, p, 0))
    return pl.pallas_call(
        functools.partial(_attn_prompt_kernel, tq=tq),
        grid=(b, NPAIR, t // tq),
        in_specs=[qspec, qspec, kspec, kspec, vspec],
        out_specs=qspec,
        out_shape=jax.ShapeDtypeStruct((b, t, D), bf16),
        scratch_shapes=[
            pltpu.VMEM((2 * tq, 2 * LANES), bf16),
            pltpu.VMEM((1, 2 * tq), f32),
            pltpu.VMEM((1, 2 * tq), f32),
            pltpu.VMEM((LANES, tq), f32),
            pltpu.VMEM((AHEAD, tq, QGRP), f32),
            pltpu.VMEM((tq, QGRP), bf16),
            pltpu.VMEM((1, QGRP), f32),
        ],
        compiler_params=_params(("parallel", "parallel", "arbitrary")),
        name="fox_attn_prompt",
    )(qm, qa, km, ka, vt)


def _attn_sample_kernel(qm_ref, qa_ref, kc_ref, kca_ref, vc_ref, kn_ref, kna_ref, vn_ref, o_ref, *, t):
    scores = []
    for pr in range(NPAIR):
        sl = slice(pr * LANES, (pr + 1) * LANES)
        qs = _stack_heads(qm_ref[0, :, sl], qa_ref[0, :, sl])
        kc = jnp.concatenate([kc_ref[0, :, sl].astype(bf16), kca_ref[0, :, sl]], axis=1)
        kn = jnp.concatenate([kn_ref[0, :, sl], kna_ref[0, :, sl]], axis=1)
        scores.append((_dot_nt(qs, kc), _dot_nt(qs, kn)))
    for pr, (s_c, s_n) in enumerate(scores):
        sl = slice(pr * LANES, (pr + 1) * LANES)
        s_n = _causal(s_n, t)
        m = jnp.maximum(jnp.max(s_c, axis=1, keepdims=True), jnp.max(s_n, axis=1, keepdims=True))
        p_c = jnp.exp2(s_c - m)
        p_n = jnp.exp2(s_n - m)
        l = jnp.sum(p_c, axis=1, keepdims=True) + jnp.sum(p_n, axis=1, keepdims=True)
        o = _dot(p_c.astype(bf16), vc_ref[0, :, sl].astype(bf16)) + _dot(p_n.astype(bf16), vn_ref[0, :, sl])
        o_ref[0, :, sl] = _unstack_heads(o * (1.0 / l), t).astype(bf16)


def _attn_sample(qm, qa, kc, kca, vc, kn, kna, vn):
    b, t, _ = qm.shape
    p = kc.shape[1]
    new = pl.BlockSpec((1, t, D), lambda bi: (bi, 0, 0))
    old = pl.BlockSpec((1, p, D), lambda bi: (bi, 0, 0))
    return pl.pallas_call(
        functools.partial(_attn_sample_kernel, t=t),
        grid=(b,),
        in_specs=[new, new, old, old, old, new, new, new],
        out_specs=new,
        out_shape=jax.ShapeDtypeStruct((b, t, D), bf16),
        compiler_params=_params(("parallel",)),
        name="fox_attn_sample",
    )(qm, qa, kc, kca, vc, kn, kna, vn)


def _trunk(x, sa, sf, past, w, consts, *, nb, tt, tq):
    b, t, _ = x.shape
    x1, nsa = _mixer(x, sa[0], w["a_g"], w["w_in"], w["a_cw"], w["w_out"], nb=nb, tt=tt)
    x2, nsf0 = _ffn(x1, sf[0], w["f_g"][0], w["w_up"][0], w["f_cw"][0], w["w_down"][0], nb=nb, tt=tt)
    if past is None:
        c0 = jnp.zeros((b, 1, LANES), f32)
    else:
        cache_k, cache_v, cache_logf = past
        kca, c0 = _cache_aug(cache_logf, consts)
    k, v, logf, qm, qa, km, ka, vm = _proj(x2, c0, w["kv_g"], w["q_g"], w["w_k"], w["w_v"], w["w_f"], w["b_f"],
                                           w["w_q"], consts, nb=nb, tt=tt, v_transposed=past is None)
    if past is None:
        o = _attn_prompt(qm, qa, km, ka, vm, tq=tq)
    else:
        p = cache_k.shape[1]
        o = _attn_sample(qm, qa, cache_k.reshape(b, p, D), kca, cache_v.reshape(b, p, D), km, ka, vm)
    y, nsf1 = _ffn(x2, sf[1], w["f_g"][1], w["w_up"][1], w["f_cw"][1], w["w_down"][1], nb=nb, tt=tt,
                   attn=(o, w["w_o"]), final_g=w["final_g"])
    return (y, nsa[None], jnp.stack([nsf0, nsf1]), k, v, logf)


def kernel(x_prompt, x_sample, state_conv_a, state_ffn_conv, cache_k, cache_v, cache_logf, a_norm, w_a_in,
           a_conv_w, w_a_out, kv_norm, w_kv, b_f, b_norm, w_q, w_o, ffn_norm, w_ffn_up, ffn_conv_w,
           w_ffn_down, final_norm):
    assert a_norm.shape[0] == 1 and b_norm.shape[0] == 1 and ffn_norm.shape[0] == 2
    w = {
        "a_g": a_norm[0][None], "w_in": w_a_in[0].astype(bf16), "a_cw": a_conv_w[0],
        "w_out": w_a_out[0].astype(bf16),
        "f_g": [ffn_norm[l][None] for l in range(2)],
        "w_up": [w_ffn_up[l].astype(bf16) for l in range(2)],
        "f_cw": [ffn_conv_w[l] for l in range(2)],
        "w_down": [w_ffn_down[l].astype(bf16) for l in range(2)],
        "kv_g": kv_norm[None], "q_g": b_norm[0][None],
        "w_k": w_kv[:, 0:D].astype(bf16), "w_v": w_kv[:, D:2 * D].astype(bf16),
        "w_f": jnp.pad(w_kv[:, 2 * D:], ((0, 0), (0, LANES - H))).astype(bf16),
        "b_f": jnp.pad(b_f, (0, LANES - H))[None],
        "w_q": (w_q[0] * (DH ** -0.5 * LOG2E)).astype(bf16),
        "w_o": w_o[0].astype(bf16), "final_g": final_norm[None],
    }
    consts = _aug_constants()
    bp = x_prompt.shape[0]
    sa0 = jnp.zeros((1, bp, 2, D), f32)
    sf0 = jnp.zeros((2, bp, 2, 2 * F), f32)
    outs_p = _trunk(x_prompt, sa0, sf0, None, w, consts, nb=1, tt=512, tq=1024)
    outs_s = _trunk(x_sample, state_conv_a, state_ffn_conv, (cache_k, cache_v, cache_logf), w, consts,
                    nb=16, tt=x_sample.shape[1], tq=None)
    y_p, p_a, p_f, p_k, p_v, p_lf = outs_p
    y_s, s_a, s_f, s_k, s_v, s_lf = outs_s
    return (y_p, y_s, p_a, p_f, p_k, p_v, p_lf, s_a, s_f, s_k, s_v, s_lf)
```
